```python
import math
import jax
import jax.numpy as jnp
from jax import lax
import numpy as np

D_MODEL = 2048
BATCH = 2
SEQ = 8192
DEPTH = 4

CHUNK = 64
RMS_EPS = 1e-6

A_HEAD_DIM = 64
A_INNER = D_MODEL
A_HEADS = A_INNER // A_HEAD_DIM
A_GROUPS = 4
A_HEADS_PER_GROUP = A_HEADS // A_GROUPS
A_STATE = 128
A_CONV_K = 4
A_CONV_DIM = A_INNER + 2 * A_GROUPS * A_STATE

B_WIDTH = D_MODEL
B_GROUP = 16
B_GROUPS = B_WIDTH // B_GROUP
B_STATE = 64
B_STEP_MIN = 1e-3
B_STEP_MAX = 1e-1

AB_IN = A_INNER + A_CONV_DIM + A_HEADS + B_WIDTH
AB_MIX = A_INNER + B_WIDTH

C_HEAD_DIM = 128
C_WIDTH = D_MODEL
C_HEADS = C_WIDTH // C_HEAD_DIM
C_BLOCK = 16

N_EXPERTS = 32
TOP_K = 4
EXPERT_FF = 3 * D_MODEL // 8
SWIGLU_LIMIT = 7.0
SWIGLU_ALPHA = 1.702

N_AB_LAYERS = (DEPTH + 1) // 2
N_C_LAYERS = DEPTH // 2

kernel_name = "hybrid_ssd_s5_hgrn2_moe_adaln"


def _rmsnorm(x, g, eps=RMS_EPS):
    xf = x.astype(jnp.float32)
    y = xf * lax.rsqrt(jnp.mean(xf * xf, axis=-1, keepdims=True) + eps)
    return (y * g.astype(jnp.float32)).astype(x.dtype)


def _modulate(h, shift, scale):
    return h * (1 + scale[:, None, :]) + shift[:, None, :]


def _causal_dwconv(u, w, b):
    k = w.shape[0]
    y = lax.conv_general_dilated(u, w[:, None, :], window_strides=(1,), padding=[(k - 1, 0)],
                                 dimension_numbers=("NWC", "WIO", "NWC"),
                                 feature_group_count=u.shape[-1])
    return y + b


def _segsum(a):
    t = a.shape[-1]
    cs = jnp.cumsum(a, axis=-1)
    diff = cs[..., :, None] - cs[..., None, :]
    mask = jnp.tril(jnp.ones((t, t), dtype=bool))
    return jnp.where(mask, diff, -jnp.inf)


def _ssd(xs, bm, cm, dt_raw, a_log, dt_bias, d_skip):
    f32 = jnp.float32
    bsz, seq, _ = xs.shape
    nc = seq // CHUNK
    x = xs.astype(f32).reshape(bsz, nc, CHUNK, A_GROUPS, A_HEADS_PER_GROUP, A_HEAD_DIM)
    bc = bm.astype(f32).reshape(bsz, nc, CHUNK, A_GROUPS, A_STATE)
    cc = cm.astype(f32).reshape(bsz, nc, CHUNK, A_GROUPS, A_STATE)
    dt = jax.nn.softplus(dt_raw.astype(f32) + dt_bias.astype(f32))
    dt = dt.reshape(bsz, nc, CHUNK, A_GROUPS, A_HEADS_PER_GROUP)
    a_head = -jnp.exp(a_log.astype(f32)).reshape(A_GROUPS, A_HEADS_PER_GROUP)
    a = jnp.transpose(dt * a_head, (0, 3, 4, 1, 2))
    xdt = x * dt[..., None]
    a_cum = jnp.cumsum(a, axis=-1)
    decay_in = jnp.exp(_segsum(a))
    cb = jnp.einsum("bclgn,bcsgn->bgcls", cc, bc)
    y_diag = jnp.einsum("bgcls,bgjcls,bcsgjp->bclgjp", cb, decay_in, xdt)
    decay_to_end = jnp.exp(a_cum[..., -1:] - a_cum)
    states = jnp.einsum("bclgn,bgjcl,bclgjp->bcgjpn", bc, decay_to_end, xdt)
    states = jnp.concatenate([jnp.zeros_like(states[:, :1]), states], axis=1)
    chunk_tot = jnp.pad(a_cum[..., -1], ((0, 0), (0, 0), (0, 0), (1, 0)))
    chunk_decay = jnp.exp(_segsum(chunk_tot))
    states = jnp.einsum("bgjzc,bcgjpn->bzgjpn", chunk_decay, states)[:, :-1]
    y_off = jnp.einsum("bclgn,bcgjpn,bgjcl->bclgjp", cc, states, jnp.exp(a_cum))
    d = d_skip.astype(f32).reshape(A_GROUPS, A_HEADS_PER_GROUP)[:, :, None]
    y = y_diag + y_off + d * x
    return y.reshape(bsz, seq, A_INNER)


def _gated_group_rmsnorm(y, z, g):
    bsz, seq, w = y.shape
    v = (y * jax.nn.silu(z.astype(jnp.float32))).reshape(bsz, seq, A_GROUPS, w // A_GROUPS)
    v = v * lax.rsqrt(jnp.mean(v * v, axis=-1, keepdims=True) + 1e-5)
    return v.reshape(bsz, seq, w) * g.astype(jnp.float32)


def _s5(u, lam_re, lam_im, log_step, b_re, b_im, c_re, c_im, d_skip, glu_w, glu_b):
    f32 = jnp.float32
    bsz, seq, _ = u.shape
    uf = u.astype(f32)
    ug = uf.reshape(bsz, seq, B_GROUPS, B_GROUP)
    lr = lam_re.astype(f32)
    li = lam_im.astype(f32)
    step = jnp.exp(log_step.astype(f32))
    mag = jnp.exp(lr * step)
    ang = li * step
    lb_re = mag * jnp.cos(ang)
    lb_im = mag * jnp.sin(ang)
    den = lr * lr + li * li
    g_re = ((lb_re - 1) * lr + lb_im * li) / den
    g_im = (lb_im * lr - (lb_re - 1) * li) / den
    br = b_re.astype(f32)
    bi = b_im.astype(f32)
    bb_re = g_re[..., None] * br - g_im[..., None] * bi
    bb_im = g_re[..., None] * bi + g_im[..., None] * br
    bu_re = jnp.einsum("blgh,gph->blgp", ug, bb_re)
    bu_im = jnp.einsum("blgh,gph->blgp", ug, bb_im)
    a_re = jnp.broadcast_to(lb_re[None, None], (1, seq, B_GROUPS, B_STATE))
    a_im = jnp.broadcast_to(lb_im[None, None], (1, seq, B_GROUPS, B_STATE))

    def combine(e1, e2):
        a1r, a1i, b1r, b1i = e1
        a2r, a2i, b2r, b2i = e2
        return (a1r * a2r - a1i * a2i, a1r * a2i + a1i * a2r,
                a2r * b1r - a2i * b1i + b2r, a2r * b1i + a2i * b1r + b2i)

    _, _, s_re, s_im = lax.associative_scan(combine, (a_re, a_im, bu_re, bu_im), axis=1)
    y = (jnp.einsum("gkp,blgp->blgk", c_re.astype(f32), s_re)
         - jnp.einsum("gkp,blgp->blgk", c_im.astype(f32), s_im))
    y = y.reshape(bsz, seq, B_WIDTH) + d_skip.astype(f32) * uf
    gate = jax.nn.gelu(y) @ glu_w.astype(f32) + glu_b.astype(f32)
    return y * jax.nn.sigmoid(gate)


def _ssd_s5_mixer(h, w_in, conv_w, conv_b, a_log, dt_bias, ssd_d, ssd_norm_g,
                  lam_re, lam_im, log_step, b_re, b_im, c_re, c_im, s5_d, glu_w, glu_b, w_out):
    proj = h @ w_in
    z, xbc, dt_raw, u = jnp.split(
        proj, [A_INNER, A_INNER + A_CONV_DIM, A_INNER + A_CONV_DIM + A_HEADS], axis=-1)
    xbc = jax.nn.silu(_causal_dwconv(xbc, conv_w, conv_b))
    xs, bm, cm = jnp.split(xbc, [A_INNER, A_INNER + A_GROUPS * A_STATE], axis=-1)
    y_a = _gated_group_rmsnorm(_ssd(xs, bm, cm, dt_raw, a_log, dt_bias, ssd_d), z, ssd_norm_g)
    y_b = _s5(u, lam_re, lam_im, log_step, b_re, b_im, c_re, c_im, s5_d, glu_w, glu_b)
    return jnp.concatenate([y_a, y_b], axis=-1).astype(h.dtype) @ w_out


def _hgrn2_mixer(h, w_in, lower_bound, norm_g, w_out):
    f32 = jnp.float32
    bsz, seq, _ = h.shape
    nb = seq // C_BLOCK
    q, f, i, og = jnp.split(h @ w_in, 4, axis=-1)
    q = jax.nn.silu(q.astype(f32))
    lb = lower_bound.astype(f32)
    forget = lb + (1 - lb) * jax.nn.sigmoid(f.astype(f32))
    k = 1 - forget
    log_f = jnp.log(forget)

    def to_blocks(t):
        return t.reshape(bsz, nb, C_BLOCK, C_HEADS, C_HEAD_DIM).transpose(1, 0, 3, 2, 4)

    mask = jnp.tril(jnp.ones((C_BLOCK, C_BLOCK), f32))

    def step(state, blk):
        qb, kb, vb, gb = blk
        cum = jnp.cumsum(gb, axis=2)
        q_dec = qb * jnp.exp(cum)
        k_inv = kb * jnp.exp(-cum)
        scores = jnp.einsum("bhtk,bhsk->bhts", q_dec, k_inv) * mask
        o = (jnp.einsum("bhts,bhsv->bhtv", scores, vb)
             + jnp.einsum("bhtk,bhkv->bhtv", q_dec, state))
        last = cum[:, :, -1:, :]
        k_end = kb * jnp.exp(last - cum)
        state = (jnp.exp(last[:, :, 0, :])[..., None] * state
                 + jnp.einsum("bhsk,bhsv->bhkv", k_end, vb))
        return state, o

    s0 = jnp.zeros((bsz, C_HEADS, C_HEAD_DIM, C_HEAD_DIM), f32)
    _, o = lax.scan(step, s0, (to_blocks(q), to_blocks(k), to_blocks(i.astype(f32)), to_blocks(log_f)))
    o = o.transpose(1, 0, 3, 2, 4).reshape(bsz, seq, C_HEADS, C_HEAD_DIM)
    o = o * lax.rsqrt(jnp.mean(o * o, axis=-1, keepdims=True) + RMS_EPS) * norm_g.astype(f32)
    o = o.reshape(bsz, seq, C_WIDTH) * jax.nn.silu(og.astype(f32))
    return o.astype(h.dtype) @ w_out


def _moe(h, router_w, router_b, w1, b1, w2, b2):
    bsz, seq, d = h.shape
    t = h.reshape(bsz * seq, d)
    logits = (t @ router_w + router_b).astype(jnp.float32)
    top_v, top_i = lax.top_k(logits, TOP_K)
    wts = jax.nn.softmax(top_v, axis=-1)
    comb = jnp.sum(jax.nn.one_hot(top_i, N_EXPERTS, dtype=jnp.float32) * wts[..., None], axis=1)
    comb = comb.astype(t.dtype)

    def expert_step(acc, e):
        w1e, b1e, w2e, b2e, ge = e
        gu = t @ w1e + b1e
        gate = jnp.minimum(gu[:, ::2], SWIGLU_LIMIT)
        up = jnp.clip(gu[:, 1::2], -SWIGLU_LIMIT, SWIGLU_LIMIT)
        act = (up + 1) * (gate * jax.nn.sigmoid(SWIGLU_ALPHA * gate))
        return acc + (ge[:, None] * (act @ w2e + b2e)).astype(acc.dtype), None

    out, _ = lax.scan(expert_step, jnp.zeros_like(t), (w1, b1, w2, b2, comb.T))
    return out.reshape(bsz, seq, d)


def setup_inputs(seed: int = 0) -> dict:
    key = jax.random.key(seed)
    keys = iter(jax.random.split(key, 40))
    f32 = jnp.float32

    def nrm(shape, scale):
        return scale * jax.random.normal(next(keys), shape, f32)

    def unif(shape, lo, hi):
        return jax.random.uniform(next(keys), shape, f32, lo, hi)

    D = D_MODEL
    NA, NC = N_AB_LAYERS, N_C_LAYERS
    dt0 = jnp.exp(unif((NA, A_HEADS), math.log(1e-3), math.log(1e-1)))
    return {
        "x": nrm((BATCH, SEQ, D), 1.0),
        "c": nrm((BATCH, D), 1.0),
        "ada_w": nrm((DEPTH, D, 6 * D), 0.5 * D ** -0.5),
        "ada_b": nrm((DEPTH, 6 * D), 0.01),
        "norm1_g": 1.0 + nrm((DEPTH, D), 0.02),
        "norm2_g": 1.0 + nrm((DEPTH, D), 0.02),
        "ab_w_in": nrm((NA, D, AB_IN), D ** -0.5),
        "ab_conv_w": nrm((NA, A_CONV_K, A_CONV_DIM), A_CONV_K ** -0.5),
        "ab_conv_b": nrm((NA, A_CONV_DIM), 0.01),
        "ssd_a_log": jnp.log(unif((NA, A_HEADS), 1.0, 16.0)),
        "ssd_dt_bias": dt0 + jnp.log(-jnp.expm1(-dt0)),
        "ssd_d": 1.0 + nrm((NA, A_HEADS), 0.02),
        "ssd_norm_g": 1.0 + nrm((NA, A_INNER), 0.02),
        "s5_lam_re": -0.5 + nrm((NA, B_GROUPS, B_STATE), 0.01),
        "s5_lam_im": jnp.pi * jnp.arange(B_STATE, dtype=f32) + nrm((NA, B_GROUPS, B_STATE), 0.01),
        "s5_log_step": unif((NA, B_GROUPS, B_STATE), math.log(B_STEP_MIN), math.log(B_STEP_MAX)),
        "s5_b_re": nrm((NA, B_GROUPS, B_STATE, B_GROUP), (2 * B_GROUP) ** -0.5),
        "s5_b_im": nrm((NA, B_GROUPS, B_STATE, B_GROUP), (2 * B_GROUP) ** -0.5),
        "s5_c_re": nrm((NA, B_GROUPS, B_GROUP, B_STATE), B_STATE ** -0.5),
        "s5_c_im": nrm((NA, B_GROUPS, B_GROUP, B_STATE), B_STATE ** -0.5),
        "s5_d": nrm((NA, B_WIDTH), 1.0),
        "s5_glu_w": nrm((NA, B_WIDTH, B_WIDTH), B_WIDTH ** -0.5),
        "s5_glu_b": nrm((NA, B_WIDTH), 0.01),
        "ab_w_out": nrm((NA, AB_MIX, D), AB_MIX ** -0.5),
        "hg_w_in": nrm((NC, D, 4 * C_WIDTH), D ** -0.5),
        "hg_lower_bounds": nrm((DEPTH, C_WIDTH), 0.1),
        "hg_norm_g": 1.0 + nrm((NC, C_HEAD_DIM), 0.02),
        "hg_w_out": nrm((NC, C_WIDTH, D), C_WIDTH ** -0.5),
        "moe_router_w": nrm((DEPTH, D, N_EXPERTS), D ** -0.5),
        "moe_router_b": nrm((DEPTH, N_EXPERTS), 0.01),
        "moe_w1": nrm((DEPTH, N_EXPERTS, D, 2 * EXPERT_FF), D ** -0.5),
        "moe_b1": nrm((DEPTH, N_EXPERTS, 2 * EXPERT_FF), 0.01),
        "moe_w2": nrm((DEPTH, N_EXPERTS, EXPERT_FF, D), EXPERT_FF ** -0.5),
        "moe_b2": nrm((DEPTH, N_EXPERTS, D), 0.01),
        "final_ada_w": nrm((D, 2 * D), 0.5 * D ** -0.5),
        "final_ada_b": nrm((2 * D,), 0.01),
        "final_norm_g": 1.0 + nrm((D,), 0.02),
    }


def reference(x, c, ada_w, ada_b, norm1_g, norm2_g,
              ab_w_in, ab_conv_w, ab_conv_b, ssd_a_log, ssd_dt_bias, ssd_d, ssd_norm_g,
              s5_lam_re, s5_lam_im, s5_log_step, s5_b_re, s5_b_im, s5_c_re, s5_c_im,
              s5_d, s5_glu_w, s5_glu_b, ab_w_out,
              hg_w_in, hg_lower_bounds, hg_norm_g, hg_w_out,
              moe_router_w, moe_router_b, moe_w1, moe_b1, moe_w2, moe_b2,
              final_ada_w, final_ada_b, final_norm_g):
    cs = jax.nn.silu(c)
    lb_soft = jax.nn.softmax(hg_lower_bounds.astype(jnp.float32), axis=0)
    lbs = jnp.cumsum(lb_soft, axis=0) - lb_soft[0]
    h = x
    for l in range(DEPTH):
        mod = cs @ ada_w[l] + ada_b[l]
        sh1, sc1, g1, sh2, sc2, g2 = jnp.split(mod, 6, axis=-1)
        hn = _modulate(_rmsnorm(h, norm1_g[l]), sh1, sc1)
        i = l // 2
        if l % 2 == 0:
            mix = _ssd_s5_mixer(hn, ab_w_in[i], ab_conv_w[i], ab_conv_b[i], ssd_a_log[i],
                                ssd_dt_bias[i], ssd_d[i], ssd_norm_g[i],
                                s5_lam_re[i], s5_lam_im[i], s5_log_step[i], s5_b_re[i], s5_b_im[i],
                                s5_c_re[i], s5_c_im[i], s5_d[i], s5_glu_w[i], s5_glu_b[i],
                                ab_w_out[i])
        else:
            mix = _hgrn2_mixer(hn, hg_w_in[i], lbs[l], hg_norm_g[i], hg_w_out[i])
        h = h + g1[:, None, :] * mix
        hn = _modulate(_rmsnorm(h, norm2_g[l]), sh2, sc2)
        h = h + g2[:, None, :] * _moe(hn, moe_router_w[l], moe_router_b[l], moe_w1[l],
                                      moe_b1[l], moe_w2[l], moe_b2[l])
    fmod = cs @ final_ada_w + final_ada_b
    f_shift, f_scale = jnp.split(fmod, 2, axis=-1)
    return _modulate(_rmsnorm(h, final_norm_g), f_shift, f_scale)
```

```python
import functools
import math

import jax
import jax.numpy as jnp
from jax import lax
from jax.experimental import pallas as pl
from jax.experimental.pallas import tpu as pltpu

F32 = jnp.float32
BF16 = jnp.bfloat16
U32 = jnp.uint32
I32 = jnp.int32

LANE = 128
VMEM_LIMIT = 56 * 1024 * 1024

D_MODEL = 2048
RMS_EPS = 1e-6

A_HEAD_DIM = 64
A_INNER = D_MODEL
A_HEADS = A_INNER // A_HEAD_DIM
A_GROUPS = 4
A_HPG = A_HEADS // A_GROUPS
A_STATE = 128
A_CONV_K = 4
A_GW = A_INNER // A_GROUPS

B_WIDTH = D_MODEL
B_GROUP = 16
B_GROUPS = B_WIDTH // B_GROUP
B_STATE = 64
S5_T = 16
S5_GPT = LANE // B_GROUP
S5_NT = B_WIDTH // LANE

C_HEAD_DIM = 128
C_WIDTH = D_MODEL
C_HEADS = C_WIDTH // C_HEAD_DIM
C_SUB = 16
C_CHUNK = 128

N_EXPERTS = 32
TOP_K = 4
EXPERT_FF = 3 * D_MODEL // 8
SWIGLU_LIMIT = 7.0
SWIGLU_ALPHA = 1.702
MOE_TM = 256

NEG = -1e30


def _cp(sem):
    return pltpu.CompilerParams(dimension_semantics=sem, vmem_limit_bytes=VMEM_LIMIT)


def _sigmoid(x):
    return 1.0 / (1.0 + jnp.exp(-x))


def _silu(x):
    return x * _sigmoid(x)


def _softplus(x):
    return jnp.maximum(x, 0.0) + jnp.log(1.0 + jnp.exp(-jnp.abs(x)))


def _gelu_tanh(x):
    return 0.5 * x * (1.0 + jnp.tanh(math.sqrt(2.0 / math.pi) * (x + 0.044715 * x * x * x)))


def _norm_mod(h, g, sh, sc):
    ms = jnp.mean(h * h, axis=-1, keepdims=True)
    return h * lax.rsqrt(ms + RMS_EPS) * g * (1.0 + sc) + sh


def _dot(a, b):
    return jnp.dot(a, b, preferred_element_type=F32)


def _dot_nt(a, b):
    return lax.dot_general(a, b, (((1,), (1,)), ((), ())), preferred_element_type=F32)


def _dot_hi(a, b):
    return jnp.dot(a, b, preferred_element_type=F32, precision=lax.Precision.HIGHEST)


def _ada_kernel(csb_ref, w_ref, b_ref, o_ref):
    nb = csb_ref.shape[0]
    tn = w_ref.shape[2]
    for q in range(tn // LANE):
        w = w_ref[0, :, q * LANE:(q + 1) * LANE]
        for b in range(nb):
            s = jnp.sum(w * csb_ref[b], axis=0, keepdims=True)
            o_ref[0, b:b + 1, q * LANE:(q + 1) * LANE] = s + b_ref[0, :, q * LANE:(q + 1) * LANE]


def _ada(c, w, bias, tn=1024):
    nl, d, n = w.shape
    nb = c.shape[0]
    cs = c * jax.nn.sigmoid(c)
    csb = jnp.broadcast_to(cs[:, :, None], (nb, d, LANE))
    return pl.pallas_call(
        _ada_kernel,
        out_shape=jax.ShapeDtypeStruct((nl, nb, n), F32),
        grid=(nl, n // tn),
        in_specs=[
            pl.BlockSpec((nb, d, LANE), lambda l, j: (0, 0, 0)),
            pl.BlockSpec((1, d, tn), lambda l, j: (l, 0, j)),
            pl.BlockSpec((1, 1, tn), lambda l, j: (l, 0, j)),
        ],
        out_specs=pl.BlockSpec((1, nb, tn), lambda l, j: (l, 0, j)),
        compiler_params=_cp(("parallel", "parallel")),
        name="ada_mod",
    )(csb, w, bias[:, None, :])


def _proj_kernel(h_ref, g_ref, sh_ref, sc_ref, w_ref, *rest, out_lt, has_extra):
    if has_extra:
        wx_ref, o_ref, ox_ref, hn_ref = rest
    else:
        o_ref, hn_ref = rest
    j = pl.program_id(2)

    @pl.when(j == 0)
    def _():
        hn = _norm_mod(h_ref[0], g_ref[...], sh_ref[0], sc_ref[0])
        hn_ref[...] = hn.astype(BF16)
        if has_extra:
            ox_ref[...] = _dot(hn_ref[...], wx_ref[...])

    acc = _dot(hn_ref[...], w_ref[...])
    if out_lt:
        for q in range(o_ref.shape[0]):
            o_ref[q] = acc[:, q * LANE:(q + 1) * LANE].astype(o_ref.dtype)
    else:
        o_ref[...] = acc.astype(o_ref.dtype)


def _proj(h, g, sh, sc, w, *, out_lt, w_extra=None, tm=1024, tn=512):
    nb, seq, d = h.shape
    n = w.shape[1]
    tm = min(tm, seq)
    rpb = seq // tm
    t = nb * seq
    has_extra = w_extra is not None
    in_specs = [
        pl.BlockSpec((1, tm, d), lambda b, i, j: (b, i, 0)),
        pl.BlockSpec((1, d), lambda b, i, j: (0, 0)),
        pl.BlockSpec((1, 1, d), lambda b, i, j: (b, 0, 0)),
        pl.BlockSpec((1, 1, d), lambda b, i, j: (b, 0, 0)),
        pl.BlockSpec((d, tn), lambda b, i, j: (0, j)),
    ]
    args = [h, g[None, :], sh[:, None, :], sc[:, None, :], w]
    if out_lt:
        out_shape = [jax.ShapeDtypeStruct((n // LANE, t, LANE), BF16)]
        out_specs = [pl.BlockSpec((tn // LANE, tm, LANE), lambda b, i, j: (j, b * rpb + i, 0))]
    else:
        out_shape = [jax.ShapeDtypeStruct((t, n), BF16)]
        out_specs = [pl.BlockSpec((tm, tn), lambda b, i, j: (b * rpb + i, j))]
    if has_extra:
        in_specs.append(pl.BlockSpec((d, LANE), lambda b, i, j: (0, 0)))
        args.append(w_extra)
        out_shape.append(jax.ShapeDtypeStruct((t, LANE), F32))
        out_specs.append(pl.BlockSpec((tm, LANE), lambda b, i, j: (b * rpb + i, 0)))
    res = pl.pallas_call(
        functools.partial(_proj_kernel, out_lt=out_lt, has_extra=has_extra),
        out_shape=out_shape,
        grid=(nb, rpb, n // tn),
        in_specs=in_specs,
        out_specs=out_specs,
        scratch_shapes=[pltpu.VMEM((tm, d), BF16)],
        compiler_params=_cp(("parallel", "parallel", "arbitrary")),
        name="proj_lt" if out_lt else "proj_std",
    )(*args)
    return res if has_extra else res[0]


def _outproj_kernel(*refs, lt_flags):
    nl = len(lt_flags)
    lhs_refs = refs[:nl]
    w_ref, h_ref, gate_ref, o_ref, a_ref = refs[nl:]
    j = pl.program_id(2)

    @pl.when(j == 0)
    def _():
        off = 0
        for r, is_lt in zip(lhs_refs, lt_flags):
            if is_lt:
                for q in range(r.shape[0]):
                    a_ref[:, off:off + LANE] = r[q]
                    off += LANE
            else:
                a_ref[:, off:off + r.shape[1]] = r[...]
                off += r.shape[1]

    acc = _dot(a_ref[...], w_ref[...])
    o_ref[0] = h_ref[0] + gate_ref[0] * acc


def _outproj(lhs, lt_flags, w, h, gate, *, tm=1024, tn=512):
    nb, seq, d = h.shape
    tm = min(tm, seq)
    rpb = seq // tm
    ktot = w.shape[0]
    in_specs = []
    for a, is_lt in zip(lhs, lt_flags):
        if is_lt:
            in_specs.append(pl.BlockSpec((a.shape[0], tm, LANE), lambda b, i, j: (0, b * rpb + i, 0)))
        else:
            in_specs.append(pl.BlockSpec((tm, a.shape[1]), lambda b, i, j: (b * rpb + i, 0)))
    in_specs += [
        pl.BlockSpec((ktot, tn), lambda b, i, j: (0, j)),
        pl.BlockSpec((1, tm, tn), lambda b, i, j: (b, i, j)),
        pl.BlockSpec((1, 1, tn), lambda b, i, j: (b, 0, j)),
    ]
    return pl.pallas_call(
        functools.partial(_outproj_kernel, lt_flags=tuple(lt_flags)),
        out_shape=jax.ShapeDtypeStruct((nb, seq, d), F32),
        grid=(nb, rpb, d // tn),
        in_specs=in_specs,
        out_specs=pl.BlockSpec((1, tm, tn), lambda b, i, j: (b, i, j)),
        scratch_shapes=[pltpu.VMEM((tm, ktot), BF16)],
        compiler_params=_cp(("parallel", "parallel", "arbitrary")),
        name="outproj",
    )(*lhs, w, h, gate[:, None, :])


def _glu_kernel(y_ref, w_ref, b_ref, o_ref, a_ref):
    j = pl.program_id(1)
    nq = o_ref.shape[1] // LANE

    @pl.when(j == 0)
    def _():
        for q in range(y_ref.shape[0]):
            a_ref[:, q * LANE:(q + 1) * LANE] = _gelu_tanh(y_ref[q].astype(F32)).astype(BF16)

    gate = _dot(a_ref[...], w_ref[...]) + b_ref[...]
    sg = _sigmoid(gate)
    for q in range(nq):
        y = y_ref[j * nq + q].astype(F32)
        o_ref[:, q * LANE:(q + 1) * LANE] = (y * sg[:, q * LANE:(q + 1) * LANE]).astype(o_ref.dtype)


def _glu(y_lt, w, b, *, tm=1024, tn=512):
    nt, t, _ = y_lt.shape
    n = w.shape[1]
    tm = min(tm, t)
    return pl.pallas_call(
        _glu_kernel,
        out_shape=jax.ShapeDtypeStruct((t, n), BF16),
        grid=(t // tm, n // tn),
        in_specs=[
            pl.BlockSpec((nt, tm, LANE), lambda i, j: (0, i, 0)),
            pl.BlockSpec((nt * LANE, tn), lambda i, j: (0, j)),
            pl.BlockSpec((1, tn), lambda i, j: (0, j)),
        ],
        out_specs=pl.BlockSpec((tm, tn), lambda i, j: (i, j)),
        scratch_shapes=[pltpu.VMEM((tm, nt * LANE), BF16)],
        compiler_params=_cp(("parallel", "arbitrary")),
        name="s5_glu",
    )(y_lt, w, b[None, :])


def _ssd_kernel(z_ref, x_ref, bm_ref, cm_ref, dt_ref, cw_ref, cb_ref, dtb_ref, ah_ref, dsk_ref, ng_ref,
                exp_ref, o_ref, ext_ref, st_ref, y_ref):
    c = pl.program_id(1)
    tc = x_ref.shape[0]
    cdim = ext_ref.shape[1]
    halo = 8

    @pl.when(c == 0)
    def _():
        ext_ref[0:halo, :] = jnp.zeros((halo, cdim), F32)
        st_ref[...] = jnp.zeros(st_ref.shape, F32)

    ext_ref[halo:halo + tc, 0:A_INNER] = x_ref[...].astype(F32)
    ext_ref[halo:halo + tc, A_INNER:A_INNER + A_GROUPS * A_STATE] = bm_ref[...].astype(F32)
    ext_ref[halo:halo + tc, A_INNER + A_GROUPS * A_STATE:cdim] = cm_ref[...].astype(F32)
    conv = cb_ref[...] + cw_ref[0:1, :] * ext_ref[halo - 3:halo - 3 + tc, :]
    for k in range(1, A_CONV_K):
        conv = conv + cw_ref[k:k + 1, :] * ext_ref[halo - 3 + k:halo - 3 + k + tc, :]
    tail = ext_ref[tc:tc + halo, :]
    ext_ref[0:halo, :] = tail
    xbc = _silu(conv)
    xs = xbc[:, 0:A_INNER]

    dt = _softplus(dt_ref[:, 0:A_HEADS] + dtb_ref[...])
    a = dt * ah_ref[...]
    row = lax.broadcasted_iota(I32, (tc, tc), 0)
    col = lax.broadcasted_iota(I32, (tc, tc), 1)
    tri = row >= col
    acum = _dot_hi(jnp.where(tri, 1.0, 0.0).astype(F32), a)
    acum_t = acum.T
    total = acum[tc - 1:tc, :]
    expand = exp_ref[...]
    dt_x = _dot_hi(dt, expand)
    dec_in = _dot_hi(jnp.exp(acum), expand)
    dec_out = _dot_hi(jnp.exp(total - acum), expand)
    dec_tot = _dot_hi(jnp.exp(total), expand)
    xdt = xs * dt_x
    xdt_b = xdt.astype(BF16)
    xend_b = (xdt * dec_out).astype(BF16)

    for g in range(A_GROUPS):
        bg = xbc[:, A_INNER + g * A_STATE:A_INNER + (g + 1) * A_STATE].astype(BF16)
        cg = xbc[:, A_INNER + A_GROUPS * A_STATE + g * A_STATE:
                 A_INNER + A_GROUPS * A_STATE + (g + 1) * A_STATE].astype(BF16)
        cb = _dot_nt(cg, bg)
        lo, hi = g * A_GW, (g + 1) * A_GW
        st = st_ref[g]
        y_off = _dot(cg, st.astype(BF16)) * dec_in[:, lo:hi]
        st_ref[g] = st * dec_tot[:, lo:hi] + _dot(bg.T, xend_b[:, lo:hi])
        y_ref[:, lo:hi] = y_off
        for jh in range(A_HPG):
            hh = g * A_HPG + jh
            d = acum[:, hh:hh + 1] - acum_t[hh:hh + 1, :]
            m = (cb * jnp.exp(jnp.where(tri, d, NEG))).astype(BF16)
            f0 = hh * A_HEAD_DIM
            y_ref[:, f0:f0 + A_HEAD_DIM] = y_ref[:, f0:f0 + A_HEAD_DIM] + _dot(m, xdt_b[:, f0:f0 + A_HEAD_DIM])

    y = y_ref[...] + dsk_ref[...] * xs
    v = y * _silu(z_ref[...].astype(F32))
    for g in range(A_GROUPS):
        lo, hi = g * A_GW, (g + 1) * A_GW
        vg = v[:, lo:hi]
        ms = jnp.mean(vg * vg, axis=-1, keepdims=True)
        o_ref[:, lo:hi] = (vg * lax.rsqrt(ms + 1e-5) * ng_ref[:, lo:hi]).astype(o_ref.dtype)


def _ssd(zxbc, dt_raw, conv_w, conv_b, a_log, dt_bias, d_skip, norm_g, nb, seq, *, tc=128):
    t = nb * seq
    tc = min(tc, seq)
    cpb = seq // tc
    cdim = A_INNER + 2 * A_GROUPS * A_STATE
    nbc = A_GROUPS * A_STATE
    expand = (jnp.arange(A_INNER)[None, :] // A_HEAD_DIM == jnp.arange(A_HEADS)[:, None]).astype(F32)
    a_head = -jnp.exp(a_log.astype(F32))[None, :]
    dsk = jnp.repeat(d_skip.astype(F32), A_HEAD_DIM)[None, :]
    const = lambda b, c: (0, 0)
    return pl.pallas_call(
        _ssd_kernel,
        out_shape=jax.ShapeDtypeStruct((t, A_INNER), BF16),
        grid=(nb, cpb),
        in_specs=[
            pl.BlockSpec((tc, A_INNER), lambda b, c: (b * cpb + c, 0)),
            pl.BlockSpec((tc, A_INNER), lambda b, c: (b * cpb + c, 1)),
            pl.BlockSpec((tc, nbc), lambda b, c: (b * cpb + c, 2 * A_INNER // nbc)),
            pl.BlockSpec((tc, nbc), lambda b, c: (b * cpb + c, 2 * A_INNER // nbc + 1)),
            pl.BlockSpec((tc, LANE), lambda b, c: (b * cpb + c, 0)),
            pl.BlockSpec((A_CONV_K, cdim), const),
            pl.BlockSpec((1, cdim), const),
            pl.BlockSpec((1, A_HEADS), const),
            pl.BlockSpec((1, A_HEADS), const),
            pl.BlockSpec((1, A_INNER), const),
            pl.BlockSpec((1, A_INNER), const),
            pl.BlockSpec((A_HEADS, A_INNER), const),
        ],
        out_specs=pl.BlockSpec((tc, A_INNER), lambda b, c: (b * cpb + c, 0)),
        scratch_shapes=[
            pltpu.VMEM((tc + 8, cdim), F32),
            pltpu.VMEM((A_GROUPS, A_STATE, A_GW), F32),
            pltpu.VMEM((tc, A_INNER), F32),
        ],
        compiler_params=_cp(("parallel", "arbitrary")),
        name="ssd",
    )(zxbc, zxbc, zxbc, zxbc, dt_raw, conv_w.astype(F32), conv_b.astype(F32)[None, :],
      dt_bias.astype(F32)[None, :], a_head, dsk, norm_g.astype(F32)[None, :], expand)


def _s5_prep(lam_re, lam_im, log_step, b_re, b_im, c_re, c_im):
    hp = lax.Precision.HIGHEST
    lr, li = lam_re.astype(F32), lam_im.astype(F32)
    step = jnp.exp(log_step.astype(F32))
    mag = jnp.exp(lr * step)
    ang = li * step
    lb_re, lb_im = mag * jnp.cos(ang), mag * jnp.sin(ang)
    den = lr * lr + li * li
    g_re = ((lb_re - 1) * lr + lb_im * li) / den
    g_im = (lb_im * lr - (lb_re - 1) * li) / den
    br, bi = b_re.astype(F32), b_im.astype(F32)
    bb_re = g_re[..., None] * br - g_im[..., None] * bi
    bb_im = g_re[..., None] * bi + g_im[..., None] * br
    cr, ci = c_re.astype(F32), c_im.astype(F32)

    def lam_pow(n):
        n = jnp.asarray(n, F32)
        m = jnp.exp(n[..., None, None] * (lr * step))
        a = n[..., None, None] * ang
        return m * jnp.cos(a), m * jnp.sin(a)

    t = S5_T
    p_re, p_im = lam_pow(jnp.arange(t + 1))
    cl_re = cr[None] * p_re[:, :, None, :] - ci[None] * p_im[:, :, None, :]
    cl_im = cr[None] * p_im[:, :, None, :] + ci[None] * p_re[:, :, None, :]
    kern = (jnp.einsum("dgkp,gph->dgkh", cl_re[:t], bb_re, precision=hp)
            - jnp.einsum("dgkp,gph->dgkh", cl_im[:t], bb_im, precision=hp))
    s_idx = jnp.arange(t)[:, None]
    t_idx = jnp.arange(t)[None, :]
    lag = t_idx - s_idx
    kt = jnp.where((lag >= 0)[:, :, None, None, None], kern[jnp.clip(lag, 0, t - 1)], 0.0)
    nt, gp = S5_NT, S5_GPT
    eye = jnp.eye(gp, dtype=F32)
    kt = kt.reshape(t, t, nt, gp, B_GROUP, B_GROUP)
    m_in = jnp.einsum("stjakh,ab->jsahtbk", kt, eye).reshape(nt, t * LANE, t * LANE)
    e_re, e_im = p_re[t - 1 - jnp.arange(t)], p_im[t - 1 - jnp.arange(t)]
    ws_re = e_re[..., None] * bb_re[None] - e_im[..., None] * bb_im[None]
    ws_im = e_re[..., None] * bb_im[None] + e_im[..., None] * bb_re[None]
    ws = jnp.stack([ws_re, ws_im], axis=0).reshape(2, t, nt, gp, B_STATE, B_GROUP)
    w_st = jnp.einsum("csjaph,ab->jsahcbp", ws, eye).reshape(nt, t * LANE, 2 * gp * B_STATE)
    w_in = jnp.concatenate([m_in, w_st], axis=2).astype(BF16)
    wo = jnp.stack([cl_re[1:t + 1], -cl_im[1:t + 1]], axis=0).reshape(2, t, nt, gp, B_GROUP, B_STATE)
    w_out = jnp.einsum("ctjakp,ab->jcaptbk", wo, eye).reshape(nt, 2 * gp * B_STATE, t * LANE).astype(BF16)
    return w_in, w_out, (lr * step, ang)


def _s5_apow(lam_log, n_steps):
    lrs, ang = lam_log
    n = (S5_T * (2 ** jnp.arange(n_steps))).astype(F32)[:, None, None]
    m = jnp.exp(n * lrs)
    ap = jnp.stack([m * jnp.cos(n * ang), m * jnp.sin(n * ang)], axis=1)
    ap = ap.reshape(n_steps, 2, S5_NT, S5_GPT * B_STATE)
    return jnp.transpose(ap, (2, 0, 1, 3)).reshape(S5_NT, n_steps * 2, S5_GPT * B_STATE)


def _s5_kernel(x_ref, win_ref, wout_ref, ap_ref, dsk_ref, o_ref, *, n_steps):
    x = x_ref[0]
    r = x.shape[0]
    nin = S5_T * LANE
    half = S5_GPT * B_STATE
    ye = _dot(x, win_ref[0])
    s_re = ye[:, nin:nin + half]
    s_im = ye[:, nin + half:nin + 2 * half]
    rows = lax.broadcasted_iota(I32, (r, half), 0)
    for k in range(n_steps):
        d = 1 << k
        if d >= r:
            break
        a_re = ap_ref[0, 2 * k:2 * k + 1, :]
        a_im = ap_ref[0, 2 * k + 1:2 * k + 2, :]
        keep = rows >= d
        p_re = jnp.where(keep, pltpu.roll(s_re, d, 0), 0.0)
        p_im = jnp.where(keep, pltpu.roll(s_im, d, 0), 0.0)
        s_re, s_im = s_re + a_re * p_re - a_im * p_im, s_im + a_re * p_im + a_im * p_re
    keep = rows >= 1
    sp_re = jnp.where(keep, pltpu.roll(s_re, 1, 0), 0.0)
    sp_im = jnp.where(keep, pltpu.roll(s_im, 1, 0), 0.0)
    y = (ye[:, 0:nin] + _dot(sp_re.astype(BF16), wout_ref[0, 0:half, :])
         + _dot(sp_im.astype(BF16), wout_ref[0, half:2 * half, :]))
    o_ref[0] = (y + dsk_ref[0] * x.astype(F32)).astype(o_ref.dtype)


def _s5(u_lt, w_in, w_out, lam_log, d_skip, nb, seq):
    nt, t, _ = u_lt.shape
    r = seq // S5_T
    n_steps = max(1, (r - 1).bit_length())
    apow = _s5_apow(lam_log, n_steps)
    x = u_lt.reshape(nt, t // S5_T, S5_T * LANE)
    dsk = jnp.tile(d_skip.astype(F32).reshape(nt, 1, LANE), (1, 1, S5_T))
    nin = S5_T * LANE
    half = S5_GPT * B_STATE
    y = pl.pallas_call(
        functools.partial(_s5_kernel, n_steps=n_steps),
        out_shape=jax.ShapeDtypeStruct(x.shape, BF16),
        grid=(nt, nb),
        in_specs=[
            pl.BlockSpec((1, r, nin), lambda j, b: (j, b, 0)),
            pl.BlockSpec((1, nin, nin + 2 * half), lambda j, b: (j, 0, 0)),
            pl.BlockSpec((1, 2 * half, nin), lambda j, b: (j, 0, 0)),
            pl.BlockSpec((1, 2 * n_steps, half), lambda j, b: (j, 0, 0)),
            pl.BlockSpec((1, 1, nin), lambda j, b: (j, 0, 0)),
        ],
        out_specs=pl.BlockSpec((1, r, nin), lambda j, b: (j, b, 0)),
        compiler_params=_cp(("parallel", "parallel")),
        name="s5",
    )(x, w_in, w_out, apow, dsk)
    return y.reshape(nt, t, LANE)


def _hgrn_levels():
    nsub = C_CHUNK // C_SUB
    levels = []
    bs = 2
    while bs <= nsub:
        levels.append((bs, [(i // bs) * bs + bs // 2 - 1 for i in range(nsub)]))
        bs *= 2
    return levels


def _hgrn_kernel(q_ref, f_ref, i_ref, og_ref, lb_ref, ng_ref, o_ref, st_ref):
    cpb = q_ref.shape[1] // C_CHUNK
    nsub = C_CHUNK // C_SUB
    dh = C_HEAD_DIM

    @pl.when(pl.program_id(2) == 0)
    def _():
        st_ref[...] = jnp.zeros(st_ref.shape, F32)

    lb = lb_ref[0]
    ng = ng_ref[...]
    row = lax.broadcasted_iota(I32, (C_CHUNK, dh), 0)
    rsub = row % C_SUB
    sub_of_row = row // C_SUB
    r2 = lax.broadcasted_iota(I32, (C_CHUNK, C_CHUNK), 0)
    c2 = lax.broadcasted_iota(I32, (C_CHUNK, C_CHUNK), 1)
    mask0 = (r2 // C_SUB == c2 // C_SUB) & (r2 >= c2)
    levels = _hgrn_levels()

    def chunk(ci, carry):
        sl = pl.ds(pl.multiple_of(ci * C_CHUNK, C_CHUNK), C_CHUNK)
        q = _silu(q_ref[0, sl, :].astype(F32))
        forget = lb + (1.0 - lb) * _sigmoid(f_ref[0, sl, :].astype(F32))
        k = 1.0 - forget
        v = i_ref[0, sl, :].astype(F32)
        g = jnp.log(forget)
        loc = g
        d = 1
        while d < C_SUB:
            loc = loc + jnp.where(rsub >= d, pltpu.roll(loc, d, 0), 0.0)
            d *= 2
        loc3 = loc.reshape(nsub, C_SUB, dh)
        last = [loc3[s, C_SUB - 1:C_SUB, :] for s in range(nsub)]
        pre = [jnp.zeros((1, dh), F32)]
        for s in range(nsub - 1):
            pre.append(pre[s] + last[s])
        ends = [pre[s] + last[s] for s in range(nsub)]
        bc = (loc3 + jnp.stack(pre, axis=0)).reshape(C_CHUNK, dh)
        total = ends[nsub - 1]

        def bcast_rows(vals):
            return jnp.broadcast_to(jnp.stack(vals, axis=0), (nsub, C_SUB, dh)).reshape(C_CHUNK, dh)

        ref0 = bcast_rows(pre)
        qd = (q * jnp.exp(bc - ref0)).astype(BF16)
        kd = (k * jnp.exp(ref0 - bc)).astype(BF16)
        scores = jnp.where(mask0, _dot_nt(qd, kd), 0.0)
        for bs, ref_sub in levels:
            ref = bcast_rows([ends[s] for s in ref_sub])
            upper = (sub_of_row % bs) >= (bs // 2)
            qd = (q * jnp.exp(jnp.where(upper, bc - ref, NEG))).astype(BF16)
            kd = (k * jnp.exp(jnp.where(upper, NEG, ref - bc))).astype(BF16)
            s_l = _dot_nt(qd, kd)
            if bs < nsub:
                s_l = jnp.where(r2 // (bs * C_SUB) == c2 // (bs * C_SUB), s_l, 0.0)
            scores = scores + s_l
        vb = v.astype(BF16)
        st_t = st_ref[...]
        q_in = (q * jnp.exp(bc)).astype(BF16)
        o = _dot(scores.astype(BF16), vb) + _dot_nt(q_in, st_t.astype(BF16))
        k_end = (k * jnp.exp(total - bc)).astype(BF16)
        st_ref[...] = st_t * jnp.exp(total) + _dot(v.T.astype(BF16), k_end)
        ms = jnp.mean(o * o, axis=-1, keepdims=True)
        out = o * lax.rsqrt(ms + RMS_EPS) * ng * _silu(og_ref[0, sl, :].astype(F32))
        o_ref[0, sl, :] = out.astype(o_ref.dtype)
        return carry

    lax.fori_loop(0, cpb, chunk, 0)


def _hgrn(p_lt, lower_bound, norm_g, nb, seq, *, tl=1024):
    nh = C_HEADS
    t = nb * seq
    tl = min(tl, seq)
    spb = seq // tl
    blk = lambda off: pl.BlockSpec((1, tl, LANE), lambda b, h, s: (off + h, b * spb + s, 0))
    return pl.pallas_call(
        _hgrn_kernel,
        out_shape=jax.ShapeDtypeStruct((nh, t, LANE), BF16),
        grid=(nb, nh, spb),
        in_specs=[blk(0), blk(nh), blk(2 * nh), blk(3 * nh),
                  pl.BlockSpec((1, 1, LANE), lambda b, h, s: (h, 0, 0)),
                  pl.BlockSpec((1, LANE), lambda b, h, s: (0, 0))],
        out_specs=pl.BlockSpec((1, tl, LANE), lambda b, h, s: (h, b * spb + s, 0)),
        scratch_shapes=[pltpu.VMEM((C_HEAD_DIM, C_HEAD_DIM), F32)],
        compiler_params=_cp(("parallel", "parallel", "arbitrary")),
        name="hgrn2",
    )(p_lt, p_lt, p_lt, p_lt, lower_bound.astype(F32).reshape(nh, 1, LANE), norm_g.astype(F32)[None, :])


def _pack_bf16_pair(lo, hi):
    lo_b = pltpu.bitcast(lo.astype(BF16).astype(F32), U32)
    hi_b = pltpu.bitcast(hi.astype(BF16).astype(F32), U32)
    return (hi_b & jnp.uint32(0xFFFF0000)) | (lo_b >> 16)


def _unpack_bf16_pair(u):
    lo = pltpu.bitcast(u << 16, F32)
    hi = pltpu.bitcast(u & jnp.uint32(0xFFFF0000), F32)
    return lo, hi


def _router_kernel(h_ref, g_ref, sh_ref, sc_ref, rw_ref, rb_ref, xq_ref, ti_ref, tw_ref):
    hn = _norm_mod(h_ref[0], g_ref[...], sh_ref[0], sc_ref[0])
    half = hn.shape[1] // 2
    xq_ref[...] = _pack_bf16_pair(hn[:, :half], hn[:, half:])
    logits = _dot_hi(hn, rw_ref[...]) + rb_ref[...]
    lane = lax.broadcasted_iota(I32, logits.shape, 1)
    lane_f = lane.astype(F32)
    vals, idxs = [], []
    for _ in range(TOP_K):
        m = jnp.max(logits, axis=-1, keepdims=True)
        idx = jnp.min(jnp.where(logits == m, lane_f, float(LANE)), axis=-1, keepdims=True)
        vals.append(m)
        idxs.append(idx)
        logits = jnp.where(lane_f == idx, NEG, logits)
    exps = [jnp.exp(v - vals[0]) for v in vals]
    den = exps[0]
    for e in exps[1:]:
        den = den + e
    ti = jnp.zeros(lane.shape, F32)
    tw = jnp.zeros(lane.shape, F32)
    for k in range(TOP_K):
        ti = jnp.where(lane == k, idxs[k], ti)
        tw = jnp.where(lane == k, exps[k] / den, tw)
    ti_ref[...] = ti.astype(I32)
    tw_ref[...] = tw


def _router(h, g, sh, sc, rw, rb, *, tm=512):
    nb, seq, d = h.shape
    tm = min(tm, seq)
    rpb = seq // tm
    t = nb * seq
    rwp = jnp.zeros((d, LANE), F32).at[:, :N_EXPERTS].set(rw.astype(F32))
    rbp = jnp.full((1, LANE), NEG, F32).at[0, :N_EXPERTS].set(rb.astype(F32))
    row = lambda b, i: (b * rpb + i, 0)
    return pl.pallas_call(
        _router_kernel,
        out_shape=[jax.ShapeDtypeStruct((t, d // 2), U32),
                   jax.ShapeDtypeStruct((t, LANE), I32),
                   jax.ShapeDtypeStruct((t, LANE), F32)],
        grid=(nb, rpb),
        in_specs=[
            pl.BlockSpec((1, tm, d), lambda b, i: (b, i, 0)),
            pl.BlockSpec((1, d), lambda b, i: (0, 0)),
            pl.BlockSpec((1, 1, d), lambda b, i: (b, 0, 0)),
            pl.BlockSpec((1, 1, d), lambda b, i: (b, 0, 0)),
            pl.BlockSpec((d, LANE), lambda b, i: (0, 0)),
            pl.BlockSpec((1, LANE), lambda b, i: (0, 0)),
        ],
        out_specs=[pl.BlockSpec((tm, d // 2), row), pl.BlockSpec((tm, LANE), row), pl.BlockSpec((tm, LANE), row)],
        compiler_params=_cp(("parallel", "parallel")),
        name="moe_router",
    )(h, g[None, :], sh[:, None, :], sc[:, None, :], rwp, rbp)


def _route_plan(top_i, tm):
    t = top_i.shape[0]
    na = t * TOP_K
    p = na + N_EXPERTS * tm
    nt = p // tm
    e_flat = top_i.reshape(na)
    oh = (e_flat[:, None] == jnp.arange(N_EXPERTS, dtype=I32)[None, :]).astype(I32)
    csum = jnp.cumsum(oh, axis=0)
    rank = jnp.take_along_axis(csum, e_flat[:, None], axis=1)[:, 0] - 1
    counts = csum[-1]
    ptiles = (counts + tm - 1) // tm
    tile_end = jnp.cumsum(ptiles)
    tile_start = tile_end - ptiles
    pos = tile_start[e_flat] * tm + rank
    a = jnp.arange(na, dtype=I32)
    src = jnp.zeros((p,), I32).at[pos].set(a // TOP_K)
    dst = (na + jnp.arange(p, dtype=I32) % tm).at[pos].set((a % TOP_K) * t + a // TOP_K)
    idx = jnp.concatenate([src.reshape(nt, tm), dst.reshape(nt, tm)], axis=1)
    tile_expert = jnp.minimum(jnp.searchsorted(tile_end, jnp.arange(nt, dtype=I32), side="right"),
                              N_EXPERTS - 1).astype(I32)
    n_valid = tile_end[-1:].astype(I32)
    return idx, tile_expert, n_valid


def _experts_kernel(te_ref, nv_ref, idx_hbm, x_hbm, w1g_ref, w1u_ref, b1g_ref, b1u_ref, w2_ref, b2_ref,
                    out_hbm, idx_sm, xbuf, ybuf, sem_idx, sem_g, sem_s):
    del te_ref
    i = pl.program_id(0)
    last = pl.num_programs(0) - 1
    nv = nv_ref[0]
    tm = xbuf.shape[1]
    half = xbuf.shape[2]

    def idx_copy(tile, slot):
        return pltpu.make_async_copy(idx_hbm.at[tile], idx_sm.at[slot], sem_idx.at[slot])

    def issue_gathers(islot, xslot):
        def body(r, c):
            tok = idx_sm[islot, r]
            pltpu.make_async_copy(x_hbm.at[pl.ds(tok, 1)], xbuf.at[xslot, pl.ds(r, 1)], sem_g.at[xslot]).start()
            return c
        lax.fori_loop(0, tm, body, 0)

    def wait_gathers(xslot):
        pltpu.make_async_copy(x_hbm.at[pl.ds(0, tm)], xbuf.at[xslot], sem_g.at[xslot]).wait()

    def issue_scatters(islot, yslot):
        def body(r, c):
            dst = idx_sm[islot, tm + r]
            pltpu.make_async_copy(ybuf.at[yslot, pl.ds(r, 1)], out_hbm.at[pl.ds(dst, 1)], sem_s.at[yslot]).start()
            return c
        lax.fori_loop(0, tm, body, 0)

    def wait_scatters(yslot):
        pltpu.make_async_copy(ybuf.at[yslot], out_hbm.at[pl.ds(0, tm)], sem_s.at[yslot]).wait()

    @pl.when(i == 0)
    def _():
        first = idx_copy(0, 0)
        first.start()
        first.wait()
        issue_gathers(0, 0)

        @pl.when(nv > 1)
        def _():
            idx_copy(1, 1).start()

    @pl.when(i + 1 < nv)
    def _():
        nslot = (i + 1) % 3
        idx_copy(i + 1, nslot).wait()
        issue_gathers(nslot, (i + 1) % 2)

        @pl.when(i + 2 < nv)
        def _():
            idx_copy(i + 2, (i + 2) % 3).start()

    @pl.when(i < nv)
    def _():
        slot = i % 2
        wait_gathers(slot)
        lo, hi = _unpack_bf16_pair(xbuf[slot])
        lo = lo.astype(BF16)
        hi = hi.astype(BF16)
        gate = _dot(lo, w1g_ref[0, 0:half, :]) + _dot(hi, w1g_ref[0, half:2 * half, :]) + b1g_ref[0]
        up = _dot(lo, w1u_ref[0, 0:half, :]) + _dot(hi, w1u_ref[0, half:2 * half, :]) + b1u_ref[0]
        gate = jnp.minimum(gate, SWIGLU_LIMIT)
        up = jnp.clip(up, -SWIGLU_LIMIT, SWIGLU_LIMIT)
        act = (up + 1.0) * (gate * _sigmoid(SWIGLU_ALPHA * gate))
        y = _dot(act.astype(BF16), w2_ref[0]) + b2_ref[0]

        @pl.when(i >= 2)
        def _():
            wait_scatters(slot)

        ybuf[slot] = _pack_bf16_pair(y[:, :half], y[:, half:])
        issue_scatters(i % 3, slot)

    @pl.when(i == last)
    def _():
        wait_scatters((nv - 1) % 2)

        @pl.when(nv > 1)
        def _():
            wait_scatters(nv % 2)


def _experts(xq, idx, tile_expert, n_valid, w1g, w1u, b1g, b1u, w2, b2, tm):
    t, half = xq.shape
    nt = idx.shape[0]
    d = 2 * half
    ff = w1g.shape[2]
    e_map3 = lambda i, te, nv: (te[i], 0, 0)
    grid_spec = pltpu.PrefetchScalarGridSpec(
        num_scalar_prefetch=2,
        grid=(nt,),
        in_specs=[
            pl.BlockSpec(memory_space=pl.ANY),
            pl.BlockSpec(memory_space=pl.ANY),
            pl.BlockSpec((1, d, ff), e_map3),
            pl.BlockSpec((1, d, ff), e_map3),
            pl.BlockSpec((1, 1, ff), e_map3),
            pl.BlockSpec((1, 1, ff), e_map3),
            pl.BlockSpec((1, ff, d), e_map3),
            pl.BlockSpec((1, 1, d), e_map3),
        ],
        out_specs=pl.BlockSpec(memory_space=pl.ANY),
        scratch_shapes=[
            pltpu.SMEM((3, 2 * tm), I32),
            pltpu.VMEM((2, tm, half), U32),
            pltpu.VMEM((2, tm, half), U32),
            pltpu.SemaphoreType.DMA((3,)),
            pltpu.SemaphoreType.DMA((2,)),
            pltpu.SemaphoreType.DMA((2,)),
        ],
    )
    return pl.pallas_call(
        _experts_kernel,
        out_shape=jax.ShapeDtypeStruct((TOP_K * t + tm, half), U32),
        grid_spec=grid_spec,
        compiler_params=_cp(("arbitrary",)),
        name="moe_experts",
    )(tile_expert, n_valid, idx, xq, w1g, w1u, b1g, b1u, w2, b2)


def _combine_kernel(y0_ref, y1_ref, y2_ref, y3_ref, tw_ref, h_ref, gate_ref, o_ref):
    half = y0_ref.shape[1]
    acc_lo = acc_hi = None
    for k, y_ref in enumerate((y0_ref, y1_ref, y2_ref, y3_ref)):
        lo, hi = _unpack_bf16_pair(y_ref[...])
        w = tw_ref[:, k:k + 1]
        acc_lo = lo * w if acc_lo is None else acc_lo + lo * w
        acc_hi = hi * w if acc_hi is None else acc_hi + hi * w
    o_ref[0, :, 0:half] = h_ref[0, :, 0:half] + gate_ref[0, :, 0:half] * acc_lo
    o_ref[0, :, half:2 * half] = h_ref[0, :, half:2 * half] + gate_ref[0, :, half:2 * half] * acc_hi


def _combine(y4, tw, h, gate, *, tm=512):
    nb, seq, d = h.shape
    tm = min(tm, seq)
    rpb = seq // tm
    t = nb * seq
    nblk = t // tm
    yspec = lambda k: pl.BlockSpec((tm, d // 2), lambda b, i: (k * nblk + b * rpb + i, 0))
    return pl.pallas_call(
        _combine_kernel,
        out_shape=jax.ShapeDtypeStruct((nb, seq, d), F32),
        grid=(nb, rpb),
        in_specs=[yspec(0), yspec(1), yspec(2), yspec(3),
                  pl.BlockSpec((tm, LANE), lambda b, i: (b * rpb + i, 0)),
                  pl.BlockSpec((1, tm, d), lambda b, i: (b, i, 0)),
                  pl.BlockSpec((1, 1, d), lambda b, i: (b, 0, 0))],
        out_specs=pl.BlockSpec((1, tm, d), lambda b, i: (b, i, 0)),
        compiler_params=_cp(("parallel", "parallel")),
        name="moe_combine",
    )(y4, y4, y4, y4, tw, h, gate[:, None, :])


def _moe_layer(h, g, sh, sc, gate, rw, rb, w1, b1, w2, b2):
    xq, ti, tw = _router(h, g, sh, sc, rw, rb)
    idx, tile_expert, n_valid = _route_plan(ti[:, :TOP_K], MOE_TM)
    w1g = w1[:, :, 0::2].astype(BF16)
    w1u = w1[:, :, 1::2].astype(BF16)
    b1g = b1[:, None, 0::2].astype(F32)
    b1u = b1[:, None, 1::2].astype(F32)
    y4 = _experts(xq, idx, tile_expert, n_valid, w1g, w1u, b1g, b1u, w2.astype(BF16), b2[:, None, :].astype(F32),
                  MOE_TM)
    return _combine(y4, tw, h, gate)


def _final_kernel(h_ref, g_ref, sh_ref, sc_ref, o_ref):
    o_ref[0] = _norm_mod(h_ref[0], g_ref[...], sh_ref[0], sc_ref[0])


def _final(h, g, sh, sc, *, tm=512):
    nb, seq, d = h.shape
    tm = min(tm, seq)
    return pl.pallas_call(
        _final_kernel,
        out_shape=jax.ShapeDtypeStruct((nb, seq, d), F32),
        grid=(nb, seq // tm),
        in_specs=[pl.BlockSpec((1, tm, d), lambda b, i: (b, i, 0)),
                  pl.BlockSpec((1, d), lambda b, i: (0, 0)),
                  pl.BlockSpec((1, 1, d), lambda b, i: (b, 0, 0)),
                  pl.BlockSpec((1, 1, d), lambda b, i: (b, 0, 0))],
        out_specs=pl.BlockSpec((1, tm, d), lambda b, i: (b, i, 0)),
        compiler_params=_cp(("parallel", "parallel")),
        name="final_norm",
    )(h, g[None, :], sh[:, None, :], sc[:, None, :])


def _ab_mixer(h, g, sh, sc, gate, w_in, conv_w, conv_b, a_log, dt_bias, ssd_d, ssd_norm_g,
              lam_re, lam_im, log_step, b_re, b_im, c_re, c_im, s5_d, glu_w, glu_b, w_out):
    nb, seq, d = h.shape
    n_zxbc = A_INNER + A_INNER + 2 * A_GROUPS * A_STATE
    w_zxbc = w_in[:, :n_zxbc].astype(BF16)
    w_dt = jnp.zeros((d, LANE), BF16).at[:, :A_HEADS].set(w_in[:, n_zxbc:n_zxbc + A_HEADS].astype(BF16))
    w_u = w_in[:, n_zxbc + A_HEADS:].astype(BF16)
    zxbc, dt_raw = _proj(h, g, sh, sc, w_zxbc, out_lt=False, w_extra=w_dt)
    u_lt = _proj(h, g, sh, sc, w_u, out_lt=True)
    y_a = _ssd(zxbc, dt_raw, conv_w, conv_b, a_log, dt_bias, ssd_d, ssd_norm_g, nb, seq)
    s5_in, s5_out, lam_log = _s5_prep(lam_re, lam_im, log_step, b_re, b_im, c_re, c_im)
    y_b = _s5(u_lt, s5_in, s5_out, lam_log, s5_d, nb, seq)
    y_b = _glu(y_b, glu_w.astype(BF16), glu_b.astype(F32))
    return _outproj([y_a, y_b], [False, False], w_out.astype(BF16), h, gate)


def _hgrn_mixer(h, g, sh, sc, gate, w_in, lower_bound, norm_g, w_out):
    nb, seq, _ = h.shape
    p_lt = _proj(h, g, sh, sc, w_in.astype(BF16), out_lt=True)
    o_lt = _hgrn(p_lt, lower_bound, norm_g, nb, seq)
    return _outproj([o_lt], [True], w_out.astype(BF16), h, gate)


def kernel(x, c, ada_w, ada_b, norm1_g, norm2_g, ab_w_in, ab_conv_w, ab_conv_b, ssd_a_log, ssd_dt_bias, ssd_d, ssd_norm_g, s5_lam_re, s5_lam_im, s5_log_step, s5_b_re, s5_b_im, s5_c_re, s5_c_im, s5_d, s5_glu_w, s5_glu_b, ab_w_out, hg_w_in, hg_lower_bounds, hg_norm_g, hg_w_out, moe_router_w, moe_router_b, moe_w1, moe_b1, moe_w2, moe_b2, final_ada_w, final_ada_b, final_norm_g):
    depth = ada_w.shape[0]
    d = x.shape[-1]
    mods = _ada(c, ada_w, ada_b)
    fmod = _ada(c, final_ada_w[None], final_ada_b[None])[0]
    lb_soft = jax.nn.softmax(hg_lower_bounds.astype(F32), axis=0)
    lbs = jnp.cumsum(lb_soft, axis=0) - lb_soft[0]
    h = x
    for l in range(depth):
        sh1, sc1, g1, sh2, sc2, g2 = [mods[l][:, k * d:(k + 1) * d] for k in range(6)]
        i = l // 2
        if l % 2 == 0:
            h = _ab_mixer(h, norm1_g[l], sh1, sc1, g1, ab_w_in[i], ab_conv_w[i], ab_conv_b[i], ssd_a_log[i],
                          ssd_dt_bias[i], ssd_d[i], ssd_norm_g[i], s5_lam_re[i], s5_lam_im[i], s5_log_step[i],
                          s5_b_re[i], s5_b_im[i], s5_c_re[i], s5_c_im[i], s5_d[i], s5_glu_w[i], s5_glu_b[i],
                          ab_w_out[i])
        else:
            h = _hgrn_mixer(h, norm1_g[l], sh1, sc1, g1, hg_w_in[i], lbs[l], hg_norm_g[i], hg_w_out[i])
        h = _moe_layer(h, norm2_g[l], sh2, sc2, g2, moe_router_w[l], moe_router_b[l], moe_w1[l], moe_b1[l],
                       moe_w2[l], moe_b2[l])
    return _final(h, final_norm_g, fmod[:, :d], fmod[:, d:])
```

```python
import functools
import math

import jax
import jax.numpy as jnp
from jax import lax
from jax.experimental import pallas as pl
from jax.experimental.pallas import tpu as pltpu

F32 = jnp.float32
BF16 = jnp.bfloat16
U32 = jnp.uint32
I32 = jnp.int32

LANE = 128
VMEM_LIMIT = 56 * 1024 * 1024

D_MODEL = 2048
RMS_EPS = 1e-6

A_HEAD_DIM = 64
A_INNER = D_MODEL
A_HEADS = A_INNER // A_HEAD_DIM
A_GROUPS = 4
A_HPG = A_HEADS // A_GROUPS
A_STATE = 128
A_CONV_K = 4
A_GW = A_INNER // A_GROUPS

B_WIDTH = D_MODEL
B_GROUP = 16
B_GROUPS = B_WIDTH // B_GROUP
B_STATE = 64
S5_T = 16
S5_GPT = LANE // B_GROUP
S5_NT = B_WIDTH // LANE

C_HEAD_DIM = 128
C_WIDTH = D_MODEL
C_HEADS = C_WIDTH // C_HEAD_DIM
C_SUB = 16
C_CHUNK = 128

N_EXPERTS = 32
TOP_K = 4
EXPERT_FF = 3 * D_MODEL // 8
SWIGLU_LIMIT = 7.0
SWIGLU_ALPHA = 1.702
MOE_TM = 256

NEG = -1e30


def _cp(sem):
    return pltpu.CompilerParams(dimension_semantics=sem, vmem_limit_bytes=VMEM_LIMIT)


def _sigmoid(x):
    return 1.0 / (1.0 + jnp.exp(-x))


def _silu(x):
    return x * _sigmoid(x)


def _softplus(x):
    return jnp.maximum(x, 0.0) + jnp.log(1.0 + jnp.exp(-jnp.abs(x)))


def _gelu_tanh(x):
    return 0.5 * x * (1.0 + jnp.tanh(math.sqrt(2.0 / math.pi) * (x + 0.044715 * x * x * x)))


def _norm_mod(h, g, sh, sc):
    ms = jnp.mean(h * h, axis=-1, keepdims=True)
    return h * lax.rsqrt(ms + RMS_EPS) * g * (1.0 + sc) + sh


def _dot(a, b):
    return jnp.dot(a, b, preferred_element_type=F32)


def _dot_nt(a, b):
    return lax.dot_general(a, b, (((1,), (1,)), ((), ())), preferred_element_type=F32)


def _dot_hi(a, b):
    return jnp.dot(a, b, preferred_element_type=F32, precision=lax.Precision.HIGHEST)


def _ada_kernel(csb_ref, w_ref, b_ref, o_ref):
    nb = csb_ref.shape[0]
    tn = w_ref.shape[2]
    for q in range(tn // LANE):
        w = w_ref[0, :, q * LANE:(q + 1) * LANE]
        for b in range(nb):
            s = jnp.sum(w * csb_ref[b], axis=0, keepdims=True)
            o_ref[0, b:b + 1, q * LANE:(q + 1) * LANE] = s + b_ref[0, :, q * LANE:(q + 1) * LANE]


def _ada(c, w, bias, tn=1024):
    nl, d, n = w.shape
    nb = c.shape[0]
    cs = c * jax.nn.sigmoid(c)
    csb = jnp.broadcast_to(cs[:, :, None], (nb, d, LANE))
    return pl.pallas_call(
        _ada_kernel,
        out_shape=jax.ShapeDtypeStruct((nl, nb, n), F32),
        grid=(nl, n // tn),
        in_specs=[
            pl.BlockSpec((nb, d, LANE), lambda l, j: (0, 0, 0)),
            pl.BlockSpec((1, d, tn), lambda l, j: (l, 0, j)),
            pl.BlockSpec((1, 1, tn), lambda l, j: (l, 0, j)),
        ],
        out_specs=pl.BlockSpec((1, nb, tn), lambda l, j: (l, 0, j)),
        compiler_params=_cp(("parallel", "parallel")),
        name="ada_mod",
    )(csb, w, bias[:, None, :])


def _proj_kernel(h_ref, g_ref, sh_ref, sc_ref, w_ref, *rest, out_lt, has_extra):
    if has_extra:
        wx_ref, o_ref, ox_ref, hn_ref = rest
    else:
        o_ref, hn_ref = rest
    j = pl.program_id(2)

    @pl.when(j == 0)
    def _():
        hn = _norm_mod(h_ref[0], g_ref[...], sh_ref[0], sc_ref[0])
        hn_ref[...] = hn.astype(BF16)
        if has_extra:
            ox_ref[...] = _dot(hn_ref[...], wx_ref[...])

    acc = _dot(hn_ref[...], w_ref[...])
    if out_lt:
        for q in range(o_ref.shape[0]):
            o_ref[q] = acc[:, q * LANE:(q + 1) * LANE].astype(o_ref.dtype)
    else:
        o_ref[...] = acc.astype(o_ref.dtype)


def _proj(h, g, sh, sc, w, *, out_lt, w_extra=None, tm=1024, tn=512):
    nb, seq, d = h.shape
    n = w.shape[1]
    tm = min(tm, seq)
    rpb = seq // tm
    t = nb * seq
    has_extra = w_extra is not None
    in_specs = [
        pl.BlockSpec((1, tm, d), lambda b, i, j: (b, i, 0)),
        pl.BlockSpec((1, d), lambda b, i, j: (0, 0)),
        pl.BlockSpec((1, 1, d), lambda b, i, j: (b, 0, 0)),
        pl.BlockSpec((1, 1, d), lambda b, i, j: (b, 0, 0)),
        pl.BlockSpec((d, tn), lambda b, i, j: (0, j)),
    ]
    args = [h, g[None, :], sh[:, None, :], sc[:, None, :], w]
    if out_lt:
        out_shape = [jax.ShapeDtypeStruct((n // LANE, t, LANE), BF16)]
        out_specs = [pl.BlockSpec((tn // LANE, tm, LANE), lambda b, i, j: (j, b * rpb + i, 0))]
    else:
        out_shape = [jax.ShapeDtypeStruct((t, n), BF16)]
        out_specs = [pl.BlockSpec((tm, tn), lambda b, i, j: (b * rpb + i, j))]
    if has_extra:
        in_specs.append(pl.BlockSpec((d, LANE), lambda b, i, j: (0, 0)))
        args.append(w_extra)
        out_shape.append(jax.ShapeDtypeStruct((t, LANE), F32))
        out_specs.append(pl.BlockSpec((tm, LANE), lambda b, i, j: (b * rpb + i, 0)))
    res = pl.pallas_call(
        functools.partial(_proj_kernel, out_lt=out_lt, has_extra=has_extra),
        out_shape=out_shape,
        grid=(nb, rpb, n // tn),
        in_specs=in_specs,
        out_specs=out_specs,
        scratch_shapes=[pltpu.VMEM((tm, d), BF16)],
        compiler_params=_cp(("parallel", "parallel", "arbitrary")),
        name="proj_lt" if out_lt else "proj_std",
    )(*args)
    return res if has_extra else res[0]


def _outproj_kernel(*refs, lt_flags):
    nl = len(lt_flags)
    lhs_refs = refs[:nl]
    w_ref, h_ref, gate_ref, o_ref, a_ref = refs[nl:]
    j = pl.program_id(2)

    @pl.when(j == 0)
    def _():
        off = 0
        for r, is_lt in zip(lhs_refs, lt_flags):
            if is_lt:
                for q in range(r.shape[0]):
                    a_ref[:, off:off + LANE] = r[q]
                    off += LANE
            else:
                a_ref[:, off:off + r.shape[1]] = r[...]
                off += r.shape[1]

    acc = _dot(a_ref[...], w_ref[...])
    o_ref[0] = h_ref[0] + gate_ref[0] * acc


def _outproj(lhs, lt_flags, w, h, gate, *, tm=1024, tn=512):
    nb, seq, d = h.shape
    tm = min(tm, seq)
    rpb = seq // tm
    ktot = w.shape[0]
    in_specs = []
    for a, is_lt in zip(lhs, lt_flags):
        if is_lt:
            in_specs.append(pl.BlockSpec((a.shape[0], tm, LANE), lambda b, i, j: (0, b * rpb + i, 0)))
        else:
            in_specs.append(pl.BlockSpec((tm, a.shape[1]), lambda b, i, j: (b * rpb + i, 0)))
    in_specs += [
        pl.BlockSpec((ktot, tn), lambda b, i, j: (0, j)),
        pl.BlockSpec((1, tm, tn), lambda b, i, j: (b, i, j)),
        pl.BlockSpec((1, 1, tn), lambda b, i, j: (b, 0, j)),
    ]
    return pl.pallas_call(
        functools.partial(_outproj_kernel, lt_flags=tuple(lt_flags)),
        out_shape=jax.ShapeDtypeStruct((nb, seq, d), F32),
        grid=(nb, rpb, d // tn),
        in_specs=in_specs,
        out_specs=pl.BlockSpec((1, tm, tn), lambda b, i, j: (b, i, j)),
        scratch_shapes=[pltpu.VMEM((tm, ktot), BF16)],
        compiler_params=_cp(("parallel", "parallel", "arbitrary")),
        name="outproj",
    )(*lhs, w, h, gate[:, None, :])


def _glu_kernel(y_ref, w_ref, b_ref, o_ref, a_ref):
    j = pl.program_id(1)
    nq = o_ref.shape[1] // LANE

    @pl.when(j == 0)
    def _():
        for q in range(y_ref.shape[0]):
            a_ref[:, q * LANE:(q + 1) * LANE] = _gelu_tanh(y_ref[q].astype(F32)).astype(BF16)

    gate = _dot(a_ref[...], w_ref[...]) + b_ref[...]
    sg = _sigmoid(gate)
    for q in range(nq):
        y = y_ref[j * nq + q].astype(F32)
        o_ref[:, q * LANE:(q + 1) * LANE] = (y * sg[:, q * LANE:(q + 1) * LANE]).astype(o_ref.dtype)


def _glu(y_lt, w, b, *, tm=1024, tn=512):
    nt, t, _ = y_lt.shape
    n = w.shape[1]
    tm = min(tm, t)
    return pl.pallas_call(
        _glu_kernel,
        out_shape=jax.ShapeDtypeStruct((t, n), BF16),
        grid=(t // tm, n // tn),
        in_specs=[
            pl.BlockSpec((nt, tm, LANE), lambda i, j: (0, i, 0)),
            pl.BlockSpec((nt * LANE, tn), lambda i, j: (0, j)),
            pl.BlockSpec((1, tn), lambda i, j: (0, j)),
        ],
        out_specs=pl.BlockSpec((tm, tn), lambda i, j: (i, j)),
        scratch_shapes=[pltpu.VMEM((tm, nt * LANE), BF16)],
        compiler_params=_cp(("parallel", "arbitrary")),
        name="s5_glu",
    )(y_lt, w, b[None, :])


def _ssd_kernel(z_ref, x_ref, bm_ref, cm_ref, dt_ref, cw_ref, cb_ref, dtb_ref, ah_ref, dsk_ref, ng_ref,
                exp_ref, o_ref, ext_ref, st_ref, y_ref):
    c = pl.program_id(1)
    tc = x_ref.shape[0]
    cdim = ext_ref.shape[1]
    halo = 8

    @pl.when(c == 0)
    def _():
        ext_ref[0:halo, :] = jnp.zeros((halo, cdim), F32)
        st_ref[...] = jnp.zeros(st_ref.shape, F32)

    ext_ref[halo:halo + tc, 0:A_INNER] = x_ref[...].astype(F32)
    ext_ref[halo:halo + tc, A_INNER:A_INNER + A_GROUPS * A_STATE] = bm_ref[...].astype(F32)
    ext_ref[halo:halo + tc, A_INNER + A_GROUPS * A_STATE:cdim] = cm_ref[...].astype(F32)
    conv = cb_ref[...] + cw_ref[0:1, :] * ext_ref[halo - 3:halo - 3 + tc, :]
    for k in range(1, A_CONV_K):
        conv = conv + cw_ref[k:k + 1, :] * ext_ref[halo - 3 + k:halo - 3 + k + tc, :]
    tail = ext_ref[tc:tc + halo, :]
    ext_ref[0:halo, :] = tail
    xbc = _silu(conv)
    xs = xbc[:, 0:A_INNER]

    dt = _softplus(dt_ref[:, 0:A_HEADS] + dtb_ref[...])
    a = dt * ah_ref[...]
    row = lax.broadcasted_iota(I32, (tc, tc), 0)
    col = lax.broadcasted_iota(I32, (tc, tc), 1)
    tri = row >= col
    acum = _dot_hi(jnp.where(tri, 1.0, 0.0).astype(F32), a)
    acum_t = acum.T
    total = acum[tc - 1:tc, :]
    expand = exp_ref[...]
    dt_x = _dot_hi(dt, expand)
    dec_in = _dot_hi(jnp.exp(acum), expand)
    dec_out = _dot_hi(jnp.exp(total - acum), expand)
    dec_tot = _dot_hi(jnp.exp(total), expand)
    xdt = xs * dt_x
    xdt_b = xdt.astype(BF16)
    xend_b = (xdt * dec_out).astype(BF16)

    for g in range(A_GROUPS):
        bg = xbc[:, A_INNER + g * A_STATE:A_INNER + (g + 1) * A_STATE].astype(BF16)
        cg = xbc[:, A_INNER + A_GROUPS * A_STATE + g * A_STATE:
                 A_INNER + A_GROUPS * A_STATE + (g + 1) * A_STATE].astype(BF16)
        cb = _dot_nt(cg, bg)
        lo, hi = g * A_GW, (g + 1) * A_GW
        st = st_ref[g]
        y_off = _dot(cg, st.astype(BF16)) * dec_in[:, lo:hi]
        st_ref[g] = st * dec_tot[:, lo:hi] + _dot(bg.T, xend_b[:, lo:hi])
        y_ref[:, lo:hi] = y_off
        for jh in range(A_HPG):
            hh = g * A_HPG + jh
            d = acum[:, hh:hh + 1] - acum_t[hh:hh + 1, :]
            m = (cb * jnp.exp(jnp.where(tri, d, NEG))).astype(BF16)
            f0 = hh * A_HEAD_DIM
            y_ref[:, f0:f0 + A_HEAD_DIM] = y_ref[:, f0:f0 + A_HEAD_DIM] + _dot(m, xdt_b[:, f0:f0 + A_HEAD_DIM])

    y = y_ref[...] + dsk_ref[...] * xs
    v = y * _silu(z_ref[...].astype(F32))
    for g in range(A_GROUPS):
        lo, hi = g * A_GW, (g + 1) * A_GW
        vg = v[:, lo:hi]
        ms = jnp.mean(vg * vg, axis=-1, keepdims=True)
        o_ref[:, lo:hi] = (vg * lax.rsqrt(ms + 1e-5) * ng_ref[:, lo:hi]).astype(o_ref.dtype)


def _ssd(zxbc, dt_raw, conv_w, conv_b, a_log, dt_bias, d_skip, norm_g, nb, seq, *, tc=128):
    t = nb * seq
    tc = min(tc, seq)
    cpb = seq // tc
    cdim = A_INNER + 2 * A_GROUPS * A_STATE
    nbc = A_GROUPS * A_STATE
    expand = (jnp.arange(A_INNER)[None, :] // A_HEAD_DIM == jnp.arange(A_HEADS)[:, None]).astype(F32)
    a_head = -jnp.exp(a_log.astype(F32))[None, :]
    dsk = jnp.repeat(d_skip.astype(F32), A_HEAD_DIM)[None, :]
    const = lambda b, c: (0, 0)
    return pl.pallas_call(
        _ssd_kernel,
        out_shape=jax.ShapeDtypeStruct((t, A_INNER), BF16),
        grid=(nb, cpb),
        in_specs=[
            pl.BlockSpec((tc, A_INNER), lambda b, c: (b * cpb + c, 0)),
            pl.BlockSpec((tc, A_INNER), lambda b, c: (b * cpb + c, 1)),
            pl.BlockSpec((tc, nbc), lambda b, c: (b * cpb + c, 2 * A_INNER // nbc)),
            pl.BlockSpec((tc, nbc), lambda b, c: (b * cpb + c, 2 * A_INNER // nbc + 1)),
            pl.BlockSpec((tc, LANE), lambda b, c: (b * cpb + c, 0)),
            pl.BlockSpec((A_CONV_K, cdim), const),
            pl.BlockSpec((1, cdim), const),
            pl.BlockSpec((1, A_HEADS), const),
            pl.BlockSpec((1, A_HEADS), const),
            pl.BlockSpec((1, A_INNER), const),
            pl.BlockSpec((1, A_INNER), const),
            pl.BlockSpec((A_HEADS, A_INNER), const),
        ],
        out_specs=pl.BlockSpec((tc, A_INNER), lambda b, c: (b * cpb + c, 0)),
        scratch_shapes=[
            pltpu.VMEM((tc + 8, cdim), F32),
            pltpu.VMEM((A_GROUPS, A_STATE, A_GW), F32),
            pltpu.VMEM((tc, A_INNER), F32),
        ],
        compiler_params=_cp(("parallel", "arbitrary")),
        name="ssd",
    )(zxbc, zxbc, zxbc, zxbc, dt_raw, conv_w.astype(F32), conv_b.astype(F32)[None, :],
      dt_bias.astype(F32)[None, :], a_head, dsk, norm_g.astype(F32)[None, :], expand)


def _s5_prep(lam_re, lam_im, log_step, b_re, b_im, c_re, c_im):
    hp = lax.Precision.HIGHEST
    lr, li = lam_re.astype(F32), lam_im.astype(F32)
    step = jnp.exp(log_step.astype(F32))
    mag = jnp.exp(lr * step)
    ang = li * step
    lb_re, lb_im = mag * jnp.cos(ang), mag * jnp.sin(ang)
    den = lr * lr + li * li
    g_re = ((lb_re - 1) * lr + lb_im * li) / den
    g_im = (lb_im * lr - (lb_re - 1) * li) / den
    br, bi = b_re.astype(F32), b_im.astype(F32)
    bb_re = g_re[..., None] * br - g_im[..., None] * bi
    bb_im = g_re[..., None] * bi + g_im[..., None] * br
    cr, ci = c_re.astype(F32), c_im.astype(F32)

    def lam_pow(n):
        n = jnp.asarray(n, F32)
        m = jnp.exp(n[..., None, None] * (lr * step))
        a = n[..., None, None] * ang
        return m * jnp.cos(a), m * jnp.sin(a)

    t = S5_T
    nt = S5_NT
    p_re, p_im = lam_pow(jnp.arange(t + 1))
    crt, cit = jnp.transpose(cr, (0, 2, 1)), jnp.transpose(ci, (0, 2, 1))
    cl_re = crt[None] * p_re[..., None] - cit[None] * p_im[..., None]
    cl_im = crt[None] * p_im[..., None] + cit[None] * p_re[..., None]
    kern = (jnp.einsum("dgpk,gph->dghk", cl_re[:t], bb_re, precision=hp)
            - jnp.einsum("dgpk,gph->dghk", cl_im[:t], bb_im, precision=hp))
    kd = kern.reshape(t, nt, LANE, B_GROUP)
    e_re, e_im = p_re[t - 1 - jnp.arange(t)], p_im[t - 1 - jnp.arange(t)]
    bbt_re, bbt_im = jnp.transpose(bb_re, (0, 2, 1)), jnp.transpose(bb_im, (0, 2, 1))
    ws_re = e_re[:, :, None, :] * bbt_re[None] - e_im[:, :, None, :] * bbt_im[None]
    ws_im = e_re[:, :, None, :] * bbt_im[None] + e_im[:, :, None, :] * bbt_re[None]
    wsc = jnp.stack([ws_re, ws_im], axis=0).reshape(2, t, nt, LANE, B_STATE)
    woc = jnp.stack([cl_re[1:t + 1], -cl_im[1:t + 1]], axis=0).reshape(2, t, nt, S5_GPT * B_STATE, B_GROUP)
    w_in, w_out = _s5_expand(kd.astype(BF16), wsc.astype(BF16), woc.astype(BF16))
    return w_in, w_out, (lr * step, ang)


def _s5_expand_kernel(kd_ref, ws_ref, wo_ref, win_ref, wout_ref):
    t = S5_T
    half = S5_GPT * B_STATE

    def expander(n_in, n_out):
        r = lax.broadcasted_iota(I32, (n_in, n_out), 0)
        c = lax.broadcasted_iota(I32, (n_in, n_out), 1)
        return jnp.where(c % n_in == r, 1.0, 0.0).astype(BF16)

    def same_group(rows, cols, rdiv, cdiv):
        r = lax.broadcasted_iota(I32, (rows, cols), 0)
        c = lax.broadcasted_iota(I32, (rows, cols), 1)
        return r // rdiv == c // cdiv

    e_k = expander(B_GROUP, LANE)
    e_p = expander(B_STATE, half)
    m_kk = same_group(LANE, LANE, B_GROUP, B_GROUP)
    m_kp = same_group(LANE, half, B_GROUP, B_STATE)
    m_pk = same_group(half, LANE, B_STATE, B_GROUP)
    zero = jnp.zeros((LANE, LANE), BF16)
    blocks = [jnp.where(m_kk, _dot(kd_ref[d, 0], e_k), 0.0).astype(BF16) for d in range(t)]
    for s in range(t):
        for u in range(t):
            win_ref[0, s * LANE:(s + 1) * LANE, u * LANE:(u + 1) * LANE] = blocks[u - s] if u >= s else zero
        for c in range(2):
            w = jnp.where(m_kp, _dot(ws_ref[c, s, 0], e_p), 0.0)
            win_ref[0, s * LANE:(s + 1) * LANE, t * LANE + c * half:t * LANE + (c + 1) * half] = w.astype(BF16)
    for c in range(2):
        for u in range(t):
            w = jnp.where(m_pk, _dot(wo_ref[c, u, 0], e_k), 0.0)
            wout_ref[0, c * half:(c + 1) * half, u * LANE:(u + 1) * LANE] = w.astype(BF16)


def _s5_expand(kd, wsc, woc):
    t, nt = S5_T, S5_NT
    half = S5_GPT * B_STATE
    nin = t * LANE
    return pl.pallas_call(
        _s5_expand_kernel,
        out_shape=[jax.ShapeDtypeStruct((nt, nin, nin + 2 * half), BF16),
                   jax.ShapeDtypeStruct((nt, 2 * half, nin), BF16)],
        grid=(nt,),
        in_specs=[pl.BlockSpec((t, 1, LANE, B_GROUP), lambda j: (0, j, 0, 0)),
                  pl.BlockSpec((2, t, 1, LANE, B_STATE), lambda j: (0, 0, j, 0, 0)),
                  pl.BlockSpec((2, t, 1, half, B_GROUP), lambda j: (0, 0, j, 0, 0))],
        out_specs=[pl.BlockSpec((1, nin, nin + 2 * half), lambda j: (j, 0, 0)),
                   pl.BlockSpec((1, 2 * half, nin), lambda j: (j, 0, 0))],
        compiler_params=_cp(("parallel",)),
        name="s5_expand",
    )(kd, wsc, woc)


def _s5_apow(lam_log, n_steps):
    lrs, ang = lam_log
    n = (S5_T * (2 ** jnp.arange(n_steps))).astype(F32)[:, None, None]
    m = jnp.exp(n * lrs)
    ap = jnp.stack([m * jnp.cos(n * ang), m * jnp.sin(n * ang)], axis=1)
    ap = ap.reshape(n_steps, 2, S5_NT, S5_GPT * B_STATE)
    return jnp.transpose(ap, (2, 0, 1, 3)).reshape(S5_NT, n_steps * 2, S5_GPT * B_STATE)


def _s5_kernel(x_ref, win_ref, wout_ref, ap_ref, dsk_ref, o_ref, *, n_steps):
    x = x_ref[0]
    r = x.shape[0]
    nin = S5_T * LANE
    half = S5_GPT * B_STATE
    ye = _dot(x, win_ref[0])
    s_re = ye[:, nin:nin + half]
    s_im = ye[:, nin + half:nin + 2 * half]
    rows = lax.broadcasted_iota(I32, (r, half), 0)
    for k in range(n_steps):
        d = 1 << k
        if d >= r:
            break
        a_re = ap_ref[0, 2 * k:2 * k + 1, :]
        a_im = ap_ref[0, 2 * k + 1:2 * k + 2, :]
        keep = rows >= d
        p_re = jnp.where(keep, pltpu.roll(s_re, d, 0), 0.0)
        p_im = jnp.where(keep, pltpu.roll(s_im, d, 0), 0.0)
        s_re, s_im = s_re + a_re * p_re - a_im * p_im, s_im + a_re * p_im + a_im * p_re
    keep = rows >= 1
    sp_re = jnp.where(keep, pltpu.roll(s_re, 1, 0), 0.0)
    sp_im = jnp.where(keep, pltpu.roll(s_im, 1, 0), 0.0)
    y = (ye[:, 0:nin] + _dot(sp_re.astype(BF16), wout_ref[0, 0:half, :])
         + _dot(sp_im.astype(BF16), wout_ref[0, half:2 * half, :]))
    o_ref[0] = (y + dsk_ref[0] * x.astype(F32)).astype(o_ref.dtype)


def _s5(u_lt, w_in, w_out, lam_log, d_skip, nb, seq):
    nt, t, _ = u_lt.shape
    r = seq // S5_T
    n_steps = max(1, (r - 1).bit_length())
    apow = _s5_apow(lam_log, n_steps)
    x = u_lt.reshape(nt, t // S5_T, S5_T * LANE)
    dsk = jnp.tile(d_skip.astype(F32).reshape(nt, 1, LANE), (1, 1, S5_T))
    nin = S5_T * LANE
    half = S5_GPT * B_STATE
    y = pl.pallas_call(
        functools.partial(_s5_kernel, n_steps=n_steps),
        out_shape=jax.ShapeDtypeStruct(x.shape, BF16),
        grid=(nt, nb),
        in_specs=[
            pl.BlockSpec((1, r, nin), lambda j, b: (j, b, 0)),
            pl.BlockSpec((1, nin, nin + 2 * half), lambda j, b: (j, 0, 0)),
            pl.BlockSpec((1, 2 * half, nin), lambda j, b: (j, 0, 0)),
            pl.BlockSpec((1, 2 * n_steps, half), lambda j, b: (j, 0, 0)),
            pl.BlockSpec((1, 1, nin), lambda j, b: (j, 0, 0)),
        ],
        out_specs=pl.BlockSpec((1, r, nin), lambda j, b: (j, b, 0)),
        compiler_params=_cp(("parallel", "parallel")),
        name="s5",
    )(x, w_in, w_out, apow, dsk)
    return y.reshape(nt, t, LANE)


def _hgrn_levels():
    nsub = C_CHUNK // C_SUB
    levels = []
    bs = 2
    while bs <= nsub:
        levels.append((bs, [(i // bs) * bs + bs // 2 - 1 for i in range(nsub)]))
        bs *= 2
    return levels


def _hgrn_kernel(q_ref, f_ref, i_ref, og_ref, lb_ref, ng_ref, o_ref, st_ref):
    cpb = q_ref.shape[1] // C_CHUNK
    nsub = C_CHUNK // C_SUB
    dh = C_HEAD_DIM

    @pl.when(pl.program_id(2) == 0)
    def _():
        st_ref[...] = jnp.zeros(st_ref.shape, F32)

    lb = lb_ref[0]
    ng = ng_ref[...]
    row = lax.broadcasted_iota(I32, (C_CHUNK, dh), 0)
    rsub = row % C_SUB
    sub_of_row = row // C_SUB
    r2 = lax.broadcasted_iota(I32, (C_CHUNK, C_CHUNK), 0)
    c2 = lax.broadcasted_iota(I32, (C_CHUNK, C_CHUNK), 1)
    mask0 = (r2 // C_SUB == c2 // C_SUB) & (r2 >= c2)
    levels = _hgrn_levels()

    def chunk(ci, carry):
        sl = pl.ds(pl.multiple_of(ci * C_CHUNK, C_CHUNK), C_CHUNK)
        q = _silu(q_ref[0, sl, :].astype(F32))
        forget = lb + (1.0 - lb) * _sigmoid(f_ref[0, sl, :].astype(F32))
        k = 1.0 - forget
        v = i_ref[0, sl, :].astype(F32)
        g = jnp.log(forget)
        loc = g
        d = 1
        while d < C_SUB:
            loc = loc + jnp.where(rsub >= d, pltpu.roll(loc, d, 0), 0.0)
            d *= 2
        loc3 = loc.reshape(nsub, C_SUB, dh)
        last = [loc3[s, C_SUB - 1:C_SUB, :] for s in range(nsub)]
        pre = [jnp.zeros((1, dh), F32)]
        for s in range(nsub - 1):
            pre.append(pre[s] + last[s])
        ends = [pre[s] + last[s] for s in range(nsub)]
        bc = (loc3 + jnp.stack(pre, axis=0)).reshape(C_CHUNK, dh)
        total = ends[nsub - 1]

        def bcast_rows(vals):
            return jnp.broadcast_to(jnp.stack(vals, axis=0), (nsub, C_SUB, dh)).reshape(C_CHUNK, dh)

        ref0 = bcast_rows(pre)
        qd = (q * jnp.exp(bc - ref0)).astype(BF16)
        kd = (k * jnp.exp(ref0 - bc)).astype(BF16)
        scores = jnp.where(mask0, _dot_nt(qd, kd), 0.0)
        for bs, ref_sub in levels:
            ref = bcast_rows([ends[s] for s in ref_sub])
            upper = (sub_of_row % bs) >= (bs // 2)
            qd = (q * jnp.exp(jnp.where(upper, bc - ref, NEG))).astype(BF16)
            kd = (k * jnp.exp(jnp.where(upper, NEG, ref - bc))).astype(BF16)
            s_l = _dot_nt(qd, kd)
            if bs < nsub:
                s_l = jnp.where(r2 // (bs * C_SUB) == c2 // (bs * C_SUB), s_l, 0.0)
            scores = scores + s_l
        vb = v.astype(BF16)
        st_t = st_ref[...]
        q_in = (q * jnp.exp(bc)).astype(BF16)
        o = _dot(scores.astype(BF16), vb) + _dot_nt(q_in, st_t.astype(BF16))
        k_end = (k * jnp.exp(total - bc)).astype(BF16)
        st_ref[...] = st_t * jnp.exp(total) + _dot(v.T.astype(BF16), k_end)
        ms = jnp.mean(o * o, axis=-1, keepdims=True)
        out = o * lax.rsqrt(ms + RMS_EPS) * ng * _silu(og_ref[0, sl, :].astype(F32))
        o_ref[0, sl, :] = out.astype(o_ref.dtype)
        return carry

    lax.fori_loop(0, cpb, chunk, 0)


def _hgrn(p_lt, lower_bound, norm_g, nb, seq, *, tl=1024):
    nh = C_HEADS
    t = nb * seq
    tl = min(tl, seq)
    spb = seq // tl
    blk = lambda off: pl.BlockSpec((1, tl, LANE), lambda b, h, s: (off + h, b * spb + s, 0))
    return pl.pallas_call(
        _hgrn_kernel,
        out_shape=jax.ShapeDtypeStruct((nh, t, LANE), BF16),
        grid=(nb, nh, spb),
        in_specs=[blk(0), blk(nh), blk(2 * nh), blk(3 * nh),
                  pl.BlockSpec((1, 1, LANE), lambda b, h, s: (h, 0, 0)),
                  pl.BlockSpec((1, LANE), lambda b, h, s: (0, 0))],
        out_specs=pl.BlockSpec((1, tl, LANE), lambda b, h, s: (h, b * spb + s, 0)),
        scratch_shapes=[pltpu.VMEM((C_HEAD_DIM, C_HEAD_DIM), F32)],
        compiler_params=_cp(("parallel", "parallel", "arbitrary")),
        name="hgrn2",
    )(p_lt, p_lt, p_lt, p_lt, lower_bound.astype(F32).reshape(nh, 1, LANE), norm_g.astype(F32)[None, :])


def _pack_bf16_pair(lo, hi):
    lo_b = pltpu.bitcast(lo.astype(BF16).astype(F32), U32)
    hi_b = pltpu.bitcast(hi.astype(BF16).astype(F32), U32)
    return (hi_b & jnp.uint32(0xFFFF0000)) | (lo_b >> 16)


def _unpack_bf16_pair(u):
    lo = pltpu.bitcast(u << 16, F32)
    hi = pltpu.bitcast(u & jnp.uint32(0xFFFF0000), F32)
    return lo, hi


ROW_SUB = 8


def _store_rows_tiled(ref, v):
    n = v.shape[0]
    for s in range(ROW_SUB):
        ref[pl.ds(s, n, stride=ROW_SUB), :] = v[:, s * LANE:(s + 1) * LANE]


def _load_rows_tiled(ref, n):
    return [ref[pl.ds(s, n, stride=ROW_SUB), :] for s in range(ROW_SUB)]


def _router_kernel(h_ref, g_ref, sh_ref, sc_ref, rw_ref, rb_ref, xq_ref, ti_ref, tw_ref, rk_ref, cnt_ref, run_ref):
    @pl.when((pl.program_id(0) == 0) & (pl.program_id(1) == 0))
    def _():
        run_ref[...] = jnp.zeros(run_ref.shape, F32)

    hn = _norm_mod(h_ref[0], g_ref[...], sh_ref[0], sc_ref[0])
    half = hn.shape[1] // 2
    _store_rows_tiled(xq_ref, _pack_bf16_pair(hn[:, :half], hn[:, half:]))
    logits = _dot_hi(hn, rw_ref[...]) + rb_ref[...]
    lane = lax.broadcasted_iota(I32, logits.shape, 1)
    lane_f = lane.astype(F32)
    vals, idxs = [], []
    for _ in range(TOP_K):
        m = jnp.max(logits, axis=-1, keepdims=True)
        idx = jnp.min(jnp.where(logits == m, lane_f, float(LANE)), axis=-1, keepdims=True)
        vals.append(m)
        idxs.append(idx)
        logits = jnp.where(lane_f == idx, NEG, logits)
    exps = [jnp.exp(v - vals[0]) for v in vals]
    den = exps[0]
    for e in exps[1:]:
        den = den + e
    tm = logits.shape[0]
    hot = [lane_f == idxs[k] for k in range(TOP_K)]
    oh = jnp.where(hot[0] | hot[1] | hot[2] | hot[3], 1.0, 0.0)
    r2 = lax.broadcasted_iota(I32, (tm, tm), 0)
    c2 = lax.broadcasted_iota(I32, (tm, tm), 1)
    before = jnp.where(r2 > c2, 1.0, 0.0).astype(BF16)
    pref = _dot(before, oh.astype(BF16)) + run_ref[...]
    ti = jnp.zeros(lane.shape, F32)
    tw = jnp.zeros(lane.shape, F32)
    rk = jnp.zeros(lane.shape, F32)
    for k in range(TOP_K):
        ti = jnp.where(lane == k, idxs[k], ti)
        tw = jnp.where(lane == k, exps[k] / den, tw)
        rk = jnp.where(lane == k, jnp.sum(jnp.where(hot[k], pref, 0.0), axis=-1, keepdims=True), rk)
    ti_ref[...] = ti.astype(I32)
    tw_ref[...] = tw
    rk_ref[...] = rk.astype(I32)
    run_ref[...] = run_ref[...] + jnp.sum(oh, axis=0, keepdims=True)
    cnt_ref[...] = run_ref[...]


def _router(h, g, sh, sc, rw, rb, *, tm=512):
    nb, seq, d = h.shape
    tm = min(tm, seq)
    rpb = seq // tm
    t = nb * seq
    rwp = jnp.zeros((d, LANE), F32).at[:, :N_EXPERTS].set(rw.astype(F32))
    rbp = jnp.full((1, LANE), NEG, F32).at[0, :N_EXPERTS].set(rb.astype(F32))
    row = lambda b, i: (b * rpb + i, 0)
    return pl.pallas_call(
        _router_kernel,
        out_shape=[jax.ShapeDtypeStruct((t * ROW_SUB, LANE), U32),
                   jax.ShapeDtypeStruct((t, LANE), I32),
                   jax.ShapeDtypeStruct((t, LANE), F32),
                   jax.ShapeDtypeStruct((t, LANE), I32),
                   jax.ShapeDtypeStruct((1, LANE), F32)],
        grid=(nb, rpb),
        in_specs=[
            pl.BlockSpec((1, tm, d), lambda b, i: (b, i, 0)),
            pl.BlockSpec((1, d), lambda b, i: (0, 0)),
            pl.BlockSpec((1, 1, d), lambda b, i: (b, 0, 0)),
            pl.BlockSpec((1, 1, d), lambda b, i: (b, 0, 0)),
            pl.BlockSpec((d, LANE), lambda b, i: (0, 0)),
            pl.BlockSpec((1, LANE), lambda b, i: (0, 0)),
        ],
        out_specs=[pl.BlockSpec((tm * ROW_SUB, LANE), row), pl.BlockSpec((tm, LANE), row), pl.BlockSpec((tm, LANE), row),
                   pl.BlockSpec((tm, LANE), row), pl.BlockSpec((1, LANE), lambda b, i: (0, 0))],
        scratch_shapes=[pltpu.VMEM((1, LANE), F32)],
        compiler_params=_cp(("arbitrary", "arbitrary")),
        name="moe_router",
    )(h, g[None, :], sh[:, None, :], sc[:, None, :], rwp, rbp)


def _route_plan(top_i, rank, counts, tm):
    t = top_i.shape[0]
    na = t * TOP_K
    nt = na // tm + N_EXPERTS + 1
    ptiles = (counts + tm - 1) // tm
    tile_end = jnp.cumsum(ptiles)
    tile_start = tile_end - ptiles
    onehot = top_i[:, :, None] == jnp.arange(N_EXPERTS, dtype=I32)[None, None, :]
    start = jnp.sum(jnp.where(onehot, tile_start[None, None, :], 0), axis=-1)
    pos = (start * tm + rank).reshape(na)
    asg = jnp.full((nt * tm,), -1, I32).at[pos].set(jnp.arange(na, dtype=I32), unique_indices=True)
    tile_expert = jnp.minimum(jnp.searchsorted(tile_end, jnp.arange(nt, dtype=I32), side="right"),
                              N_EXPERTS - 1).astype(I32)
    n_valid = tile_end[-1:].astype(I32)
    return asg.reshape(nt, tm), tile_expert, n_valid


def _w1_prep_kernel(w_ref, p_ref, o_ref):
    nblk = o_ref.shape[1]
    wblk = o_ref.shape[3]
    for b in range(nblk):
        o_ref[0, b] = _dot(w_ref[0, :, b * wblk:(b + 1) * wblk].astype(BF16), p_ref[...]).astype(BF16)


def _w1_prep(w1, *, tr=512):
    e, d, n2 = w1.shape
    wblk = 2 * LANE
    nblk = n2 // wblk
    r = jnp.arange(wblk)
    perm = (jnp.where(r % 2 == 0, r // 2, LANE + r // 2)[:, None] == jnp.arange(wblk)[None, :]).astype(BF16)
    return pl.pallas_call(
        _w1_prep_kernel,
        out_shape=jax.ShapeDtypeStruct((e, nblk, d, wblk), BF16),
        grid=(e, d // tr),
        in_specs=[pl.BlockSpec((1, tr, n2), lambda x, i: (x, i, 0)),
                  pl.BlockSpec((wblk, wblk), lambda x, i: (0, 0))],
        out_specs=pl.BlockSpec((1, nblk, tr, wblk), lambda x, i: (x, 0, i, 0)),
        compiler_params=_cp(("parallel", "parallel")),
        name="moe_w1_prep",
    )(w1, perm)


def _experts_kernel(te_ref, nv_ref, asg_hbm, x_hbm, w1_ref, b1_ref, w2_ref, b2_ref, out_hbm,
                    asg_sm, xbuf, ybuf, sem_idx, sem_g, sem_s, *, n_tok):
    del te_ref
    i = pl.program_id(0)
    nt = pl.num_programs(0)
    nv = nv_ref[0]
    rs = ROW_SUB
    tm = xbuf.shape[1] // rs
    half = rs * LANE
    nblk = w1_ref.shape[1]
    na = TOP_K * n_tok

    def row_tile(row):
        return pl.ds(pl.multiple_of(row * rs, rs), rs)

    def dest_row(a, r):
        return jnp.where(a >= 0, (a & (TOP_K - 1)) * n_tok + (a >> 2), na + r)

    def asg_copy(tile, slot):
        return pltpu.make_async_copy(asg_hbm.at[tile], asg_sm.at[slot], sem_idx.at[slot])

    def gather_start(islot, xslot, r):
        tok = jnp.maximum(asg_sm[islot, r], 0) >> 2
        pltpu.make_async_copy(x_hbm.at[row_tile(tok)], xbuf.at[xslot, row_tile(r)], sem_g.at[xslot]).start()

    def scatter_start(islot, yslot, r):
        dst = dest_row(asg_sm[islot, r], r)
        pltpu.make_async_copy(ybuf.at[yslot, row_tile(r)], out_hbm.at[row_tile(dst)], sem_s.at[yslot]).start()

    def wait_gathers(xslot):
        pltpu.make_async_copy(x_hbm.at[pl.ds(0, tm * rs)], xbuf.at[xslot], sem_g.at[xslot]).wait()

    def wait_scatters(yslot):
        pltpu.make_async_copy(ybuf.at[yslot], out_hbm.at[pl.ds(0, tm * rs)], sem_s.at[yslot]).wait()

    @pl.when(i == 0)
    def _():
        ybuf[...] = jnp.zeros(ybuf.shape, U32)
        init = pltpu.make_async_copy(ybuf.at[0], out_hbm.at[pl.ds(na * rs, tm * rs)], sem_s.at[0])
        init.start()
        init.wait()

        def fill(r, c):
            asg_sm[3, r] = -1
            return c
        lax.fori_loop(0, tm, fill, 0)
        first = asg_copy(0, 0)
        first.start()
        first.wait()

        def body(r, c):
            gather_start(0, 0, r)
            return c
        lax.fori_loop(0, tm, body, 0)
        asg_copy(1, 1).start()

    @pl.when(i < nv)
    def _():
        slot = i % 2
        asg_copy(i + 1, (i + 1) % 4).wait()

        @pl.when(i + 2 <= nv)
        def _():
            asg_copy(i + 2, (i + 2) % 4).start()

        wait_gathers(slot)
        pieces = [_unpack_bf16_pair(p) for p in _load_rows_tiled(xbuf.at[slot], tm)]
        xb = jnp.concatenate([p[0].astype(BF16) for p in pieces] + [p[1].astype(BF16) for p in pieces], axis=1)
        g_slot, s_slot = (i + 1) % 4, (i + 3) % 4
        rows_per_blk = -(-tm // nblk)
        acts = []
        zmask = (nv >> 30).astype(U32)
        for b in range(nblk):
            for r in range(b * rows_per_blk, min((b + 1) * rows_per_blk, tm)):
                gather_start(g_slot, 1 - slot, r)
                scatter_start(s_slot, 1 - slot, r)
            probe = pltpu.bitcast(xbuf[slot, 0:1, :] & zmask, F32)
            gu = _dot(xb, w1_ref[0, b]) + b1_ref[0, b]
            gate = jnp.minimum(gu[:, 0:LANE] + probe, SWIGLU_LIMIT)
            up = jnp.clip(gu[:, LANE:2 * LANE], -SWIGLU_LIMIT, SWIGLU_LIMIT)
            acts.append(((up + 1.0) * (gate * _sigmoid(SWIGLU_ALPHA * gate))).astype(BF16))
        y = _dot(jnp.concatenate(acts, axis=1), w2_ref[0]) + b2_ref[0]

        @pl.when(i >= 1)
        def _():
            wait_scatters(slot)

        _store_rows_tiled(ybuf.at[slot], _pack_bf16_pair(y[:, :half], y[:, half:]))

    @pl.when(i == nv)
    def _():
        wait_gathers(nv % 2)
        yslot = (nv + 1) % 2
        islot = (nv + 3) % 4

        def body(r, c):
            scatter_start(islot, yslot, r)
            return c
        lax.fori_loop(0, tm, body, 0)
        wait_scatters(0)
        wait_scatters(1)


def _experts(xq, asg, tile_expert, n_valid, w1p, b1p, w2, b2, tm):
    t = xq.shape[0] // ROW_SUB
    nt = asg.shape[0]
    d = 2 * ROW_SUB * LANE
    nblk, wblk = w1p.shape[1], w1p.shape[3]
    e_map4 = lambda i, te, nv: (te[i], 0, 0, 0)
    grid_spec = pltpu.PrefetchScalarGridSpec(
        num_scalar_prefetch=2,
        grid=(nt,),
        in_specs=[
            pl.BlockSpec(memory_space=pl.ANY),
            pl.BlockSpec(memory_space=pl.ANY),
            pl.BlockSpec((1, nblk, d, wblk), e_map4),
            pl.BlockSpec((1, nblk, 1, wblk), e_map4),
            pl.BlockSpec((1, nblk * LANE, d), lambda i, te, nv: (te[i], 0, 0)),
            pl.BlockSpec((1, 1, d), lambda i, te, nv: (te[i], 0, 0)),
        ],
        out_specs=pl.BlockSpec(memory_space=pl.ANY),
        scratch_shapes=[
            pltpu.SMEM((4, tm), I32),
            pltpu.VMEM((2, tm * ROW_SUB, LANE), U32),
            pltpu.VMEM((2, tm * ROW_SUB, LANE), U32),
            pltpu.SemaphoreType.DMA((4,)),
            pltpu.SemaphoreType.DMA((2,)),
            pltpu.SemaphoreType.DMA((2,)),
        ],
    )
    return pl.pallas_call(
        functools.partial(_experts_kernel, n_tok=t),
        out_shape=jax.ShapeDtypeStruct(((TOP_K * t + tm) * ROW_SUB, LANE), U32),
        grid_spec=grid_spec,
        compiler_params=_cp(("arbitrary",)),
        name="moe_experts",
    )(tile_expert, n_valid, asg, xq, w1p, b1p, w2, b2)


def _combine_kernel(y0_ref, y1_ref, y2_ref, y3_ref, tw_ref, h_ref, gate_ref, o_ref):
    tm = tw_ref.shape[0]
    half = ROW_SUB * LANE
    ws = [tw_ref[:, k:k + 1] for k in range(TOP_K)]
    for s in range(ROW_SUB):
        acc_lo = acc_hi = None
        for k, y_ref in enumerate((y0_ref, y1_ref, y2_ref, y3_ref)):
            lo, hi = _unpack_bf16_pair(y_ref[pl.ds(s, tm, stride=ROW_SUB), :])
            acc_lo = lo * ws[k] if acc_lo is None else acc_lo + lo * ws[k]
            acc_hi = hi * ws[k] if acc_hi is None else acc_hi + hi * ws[k]
        c0, c1 = s * LANE, half + s * LANE
        o_ref[0, :, c0:c0 + LANE] = h_ref[0, :, c0:c0 + LANE] + gate_ref[0, :, c0:c0 + LANE] * acc_lo
        o_ref[0, :, c1:c1 + LANE] = h_ref[0, :, c1:c1 + LANE] + gate_ref[0, :, c1:c1 + LANE] * acc_hi


def _combine(y4, tw, h, gate, *, tm=512):
    nb, seq, d = h.shape
    tm = min(tm, seq)
    rpb = seq // tm
    t = nb * seq
    nblk = t // tm
    yspec = lambda k: pl.BlockSpec((tm * ROW_SUB, LANE), lambda b, i: (k * nblk + b * rpb + i, 0))
    return pl.pallas_call(
        _combine_kernel,
        out_shape=jax.ShapeDtypeStruct((nb, seq, d), F32),
        grid=(nb, rpb),
        in_specs=[yspec(0), yspec(1), yspec(2), yspec(3),
                  pl.BlockSpec((tm, LANE), lambda b, i: (b * rpb + i, 0)),
                  pl.BlockSpec((1, tm, d), lambda b, i: (b, i, 0)),
                  pl.BlockSpec((1, 1, d), lambda b, i: (b, 0, 0))],
        out_specs=pl.BlockSpec((1, tm, d), lambda b, i: (b, i, 0)),
        compiler_params=_cp(("parallel", "parallel")),
        name="moe_combine",
    )(y4, y4, y4, y4, tw, h, gate[:, None, :])


def _moe_layer(h, g, sh, sc, gate, rw, rb, w1, b1, w2, b2):
    xq, ti, tw, rk, cnt = _router(h, g, sh, sc, rw, rb)
    counts = cnt[0, :N_EXPERTS].astype(I32)
    asg, tile_expert, n_valid = _route_plan(ti[:, :TOP_K], rk[:, :TOP_K], counts, MOE_TM)
    ne, ff, d = w2.shape
    nblk = ff // LANE
    w1p = _w1_prep(w1)
    b1p = b1.astype(F32).reshape(ne, nblk, LANE, 2).transpose(0, 1, 3, 2).reshape(ne, nblk, 1, 2 * LANE)
    w2p = w2.astype(BF16)
    y4 = _experts(xq, asg, tile_expert, n_valid, w1p, b1p, w2p, b2[:, None, :].astype(F32), MOE_TM)
    return _combine(y4, tw, h, gate)


def _final_kernel(h_ref, g_ref, sh_ref, sc_ref, o_ref):
    o_ref[0] = _norm_mod(h_ref[0], g_ref[...], sh_ref[0], sc_ref[0])


def _final(h, g, sh, sc, *, tm=512):
    nb, seq, d = h.shape
    tm = min(tm, seq)
    return pl.pallas_call(
        _final_kernel,
        out_shape=jax.ShapeDtypeStruct((nb, seq, d), F32),
        grid=(nb, seq // tm),
        in_specs=[pl.BlockSpec((1, tm, d), lambda b, i: (b, i, 0)),
                  pl.BlockSpec((1, d), lambda b, i: (0, 0)),
                  pl.BlockSpec((1, 1, d), lambda b, i: (b, 0, 0)),
                  pl.BlockSpec((1, 1, d), lambda b, i: (b, 0, 0))],
        out_specs=pl.BlockSpec((1, tm, d), lambda b, i: (b, i, 0)),
        compiler_params=_cp(("parallel", "parallel")),
        name="final_norm",
    )(h, g[None, :], sh[:, None, :], sc[:, None, :])


def _ab_mixer(h, g, sh, sc, gate, w_in, conv_w, conv_b, a_log, dt_bias, ssd_d, ssd_norm_g,
              lam_re, lam_im, log_step, b_re, b_im, c_re, c_im, s5_d, glu_w, glu_b, w_out):
    nb, seq, d = h.shape
    n_zxbc = A_INNER + A_INNER + 2 * A_GROUPS * A_STATE
    w_zxbc = w_in[:, :n_zxbc].astype(BF16)
    w_dt = jnp.zeros((d, LANE), BF16).at[:, :A_HEADS].set(w_in[:, n_zxbc:n_zxbc + A_HEADS].astype(BF16))
    w_u = w_in[:, n_zxbc + A_HEADS:].astype(BF16)
    zxbc, dt_raw = _proj(h, g, sh, sc, w_zxbc, out_lt=False, w_extra=w_dt)
    u_lt = _proj(h, g, sh, sc, w_u, out_lt=True)
    y_a = _ssd(zxbc, dt_raw, conv_w, conv_b, a_log, dt_bias, ssd_d, ssd_norm_g, nb, seq)
    s5_in, s5_out, lam_log = _s5_prep(lam_re, lam_im, log_step, b_re, b_im, c_re, c_im)
    y_b = _s5(u_lt, s5_in, s5_out, lam_log, s5_d, nb, seq)
    y_b = _glu(y_b, glu_w.astype(BF16), glu_b.astype(F32))
    return _outproj([y_a, y_b], [False, False], w_out.astype(BF16), h, gate)


def _hgrn_mixer(h, g, sh, sc, gate, w_in, lower_bound, norm_g, w_out):
    nb, seq, _ = h.shape
    p_lt = _proj(h, g, sh, sc, w_in.astype(BF16), out_lt=True)
    o_lt = _hgrn(p_lt, lower_bound, norm_g, nb, seq)
    return _outproj([o_lt], [True], w_out.astype(BF16), h, gate)


def kernel(x, c, ada_w, ada_b, norm1_g, norm2_g, ab_w_in, ab_conv_w, ab_conv_b, ssd_a_log, ssd_dt_bias, ssd_d, ssd_norm_g, s5_lam_re, s5_lam_im, s5_log_step, s5_b_re, s5_b_im, s5_c_re, s5_c_im, s5_d, s5_glu_w, s5_glu_b, ab_w_out, hg_w_in, hg_lower_bounds, hg_norm_g, hg_w_out, moe_router_w, moe_router_b, moe_w1, moe_b1, moe_w2, moe_b2, final_ada_w, final_ada_b, final_norm_g):
    depth = ada_w.shape[0]
    d = x.shape[-1]
    mods = _ada(c, ada_w, ada_b)
    fmod = _ada(c, final_ada_w[None], final_ada_b[None])[0]
    lb_soft = jax.nn.softmax(hg_lower_bounds.astype(F32), axis=0)
    lbs = jnp.cumsum(lb_soft, axis=0) - lb_soft[0]
    h = x
    for l in range(depth):
        sh1, sc1, g1, sh2, sc2, g2 = [mods[l][:, k * d:(k + 1) * d] for k in range(6)]
        i = l // 2
        if l % 2 == 0:
            h = _ab_mixer(h, norm1_g[l], sh1, sc1, g1, ab_w_in[i], ab_conv_w[i], ab_conv_b[i], ssd_a_log[i],
                          ssd_dt_bias[i], ssd_d[i], ssd_norm_g[i], s5_lam_re[i], s5_lam_im[i], s5_log_step[i],
                          s5_b_re[i], s5_b_im[i], s5_c_re[i], s5_c_im[i], s5_d[i], s5_glu_w[i], s5_glu_b[i],
                          ab_w_out[i])
        else:
            h = _hgrn_mixer(h, norm1_g[l], sh1, sc1, g1, hg_w_in[i], lbs[l], hg_norm_g[i], hg_w_out[i])
        h = _moe_layer(h, norm2_g[l], sh2, sc2, g2, moe_router_w[l], moe_router_b[l], moe_w1[l], moe_b1[l],
                       moe_w2[l], moe_b2[l])
    return _final(h, final_norm_g, fmod[:, :d], fmod[:, d:])
```

```python
import functools
import math

import jax
import jax.numpy as jnp
from jax import lax
from jax.experimental import pallas as pl
from jax.experimental.pallas import tpu as pltpu

F32 = jnp.float32
BF16 = jnp.bfloat16
U32 = jnp.uint32
I32 = jnp.int32

LANE = 128
VMEM_LIMIT = 56 * 1024 * 1024

D_MODEL = 2048
RMS_EPS = 1e-6

A_HEAD_DIM = 64
A_INNER = D_MODEL
A_HEADS = A_INNER // A_HEAD_DIM
A_GROUPS = 4
A_HPG = A_HEADS // A_GROUPS
A_STATE = 128
A_CONV_K = 4
A_GW = A_INNER // A_GROUPS

B_WIDTH = D_MODEL
B_GROUP = 16
B_GROUPS = B_WIDTH // B_GROUP
B_STATE = 64
S5_T = 16
S5_GPT = LANE // B_GROUP
S5_NT = B_WIDTH // LANE

C_HEAD_DIM = 128
C_WIDTH = D_MODEL
C_HEADS = C_WIDTH // C_HEAD_DIM
C_SUB = 16
C_CHUNK = 128

N_EXPERTS = 32
TOP_K = 4
EXPERT_FF = 3 * D_MODEL // 8
SWIGLU_LIMIT = 7.0
SWIGLU_ALPHA = 1.702
MOE_TM = 256

NEG = -1e30


def _cp(sem):
    return pltpu.CompilerParams(dimension_semantics=sem, vmem_limit_bytes=VMEM_LIMIT)


def _sigmoid(x):
    return 1.0 / (1.0 + jnp.exp(-x))


def _silu(x):
    return x * _sigmoid(x)


def _softplus(x):
    return jnp.maximum(x, 0.0) + jnp.log(1.0 + jnp.exp(-jnp.abs(x)))


def _gelu_tanh(x):
    return 0.5 * x * (1.0 + jnp.tanh(math.sqrt(2.0 / math.pi) * (x + 0.044715 * x * x * x)))


def _norm_mod(h, g, sh, sc):
    ms = jnp.mean(h * h, axis=-1, keepdims=True)
    return h * lax.rsqrt(ms + RMS_EPS) * g * (1.0 + sc) + sh


def _dot(a, b):
    return jnp.dot(a, b, preferred_element_type=F32)


def _dot_nt(a, b):
    return lax.dot_general(a, b, (((1,), (1,)), ((), ())), preferred_element_type=F32)


def _split3(x):
    p1 = x.astype(BF16)
    r1 = x - p1.astype(F32)
    p2 = r1.astype(BF16)
    p3 = (r1 - p2.astype(F32)).astype(BF16)
    return jnp.concatenate([p1, p2, p3], axis=1)


def _dot_hi(a, b):
    return jnp.dot(a, b, preferred_element_type=F32, precision=lax.Precision.HIGHEST)


def _ada_kernel(csb_ref, w_ref, b_ref, o_ref):
    nb = csb_ref.shape[0]
    tn = w_ref.shape[2]
    for q in range(tn // LANE):
        w = w_ref[0, :, q * LANE:(q + 1) * LANE]
        for b in range(nb):
            s = jnp.sum(w * csb_ref[b], axis=0, keepdims=True)
            o_ref[0, b:b + 1, q * LANE:(q + 1) * LANE] = s + b_ref[0, :, q * LANE:(q + 1) * LANE]


def _ada(c, w, bias, tn=1024):
    nl, d, n = w.shape
    nb = c.shape[0]
    cs = c * jax.nn.sigmoid(c)
    csb = jnp.broadcast_to(cs[:, :, None], (nb, d, LANE))
    return pl.pallas_call(
        _ada_kernel,
        out_shape=jax.ShapeDtypeStruct((nl, nb, n), F32),
        grid=(nl, n // tn),
        in_specs=[
            pl.BlockSpec((nb, d, LANE), lambda l, j: (0, 0, 0)),
            pl.BlockSpec((1, d, tn), lambda l, j: (l, 0, j)),
            pl.BlockSpec((1, 1, tn), lambda l, j: (l, 0, j)),
        ],
        out_specs=pl.BlockSpec((1, nb, tn), lambda l, j: (l, 0, j)),
        compiler_params=_cp(("parallel", "parallel")),
        name="ada_mod",
    )(csb, w, bias[:, None, :])


def _proj_kernel(h_ref, g_ref, sh_ref, sc_ref, w_ref, *rest, out_lt, has_extra):
    if has_extra:
        wx_ref, o_ref, ox_ref, hn_ref = rest
    else:
        o_ref, hn_ref = rest
    j = pl.program_id(2)

    @pl.when(j == 0)
    def _():
        hn = _norm_mod(h_ref[0], g_ref[...], sh_ref[0], sc_ref[0])
        hn_ref[...] = hn.astype(BF16)
        if has_extra:
            ox_ref[...] = _dot(hn_ref[...], wx_ref[...])

    acc = _dot(hn_ref[...], w_ref[...])
    if out_lt:
        for q in range(o_ref.shape[0]):
            o_ref[q] = acc[:, q * LANE:(q + 1) * LANE].astype(o_ref.dtype)
    else:
        o_ref[...] = acc.astype(o_ref.dtype)


def _proj(h, g, sh, sc, w, *, out_lt, w_extra=None, tm=1024, tn=512):
    nb, seq, d = h.shape
    n = w.shape[1]
    tm = min(tm, seq)
    rpb = seq // tm
    t = nb * seq
    has_extra = w_extra is not None
    in_specs = [
        pl.BlockSpec((1, tm, d), lambda b, i, j: (b, i, 0)),
        pl.BlockSpec((1, d), lambda b, i, j: (0, 0)),
        pl.BlockSpec((1, 1, d), lambda b, i, j: (b, 0, 0)),
        pl.BlockSpec((1, 1, d), lambda b, i, j: (b, 0, 0)),
        pl.BlockSpec((d, tn), lambda b, i, j: (0, j)),
    ]
    args = [h, g[None, :], sh[:, None, :], sc[:, None, :], w]
    if out_lt:
        out_shape = [jax.ShapeDtypeStruct((n // LANE, t, LANE), BF16)]
        out_specs = [pl.BlockSpec((tn // LANE, tm, LANE), lambda b, i, j: (j, b * rpb + i, 0))]
    else:
        out_shape = [jax.ShapeDtypeStruct((t, n), BF16)]
        out_specs = [pl.BlockSpec((tm, tn), lambda b, i, j: (b * rpb + i, j))]
    if has_extra:
        in_specs.append(pl.BlockSpec((d, LANE), lambda b, i, j: (0, 0)))
        args.append(w_extra)
        out_shape.append(jax.ShapeDtypeStruct((t, LANE), F32))
        out_specs.append(pl.BlockSpec((tm, LANE), lambda b, i, j: (b * rpb + i, 0)))
    res = pl.pallas_call(
        functools.partial(_proj_kernel, out_lt=out_lt, has_extra=has_extra),
        out_shape=out_shape,
        grid=(nb, rpb, n // tn),
        in_specs=in_specs,
        out_specs=out_specs,
        scratch_shapes=[pltpu.VMEM((tm, d), BF16)],
        compiler_params=_cp(("parallel", "parallel", "arbitrary")),
        name="proj_lt" if out_lt else "proj_std",
    )(*args)
    return res if has_extra else res[0]


def _outproj_kernel(*refs, lt_flags):
    nl = len(lt_flags)
    lhs_refs = refs[:nl]
    w_ref, h_ref, gate_ref, o_ref, a_ref = refs[nl:]
    j = pl.program_id(2)

    @pl.when(j == 0)
    def _():
        off = 0
        for r, is_lt in zip(lhs_refs, lt_flags):
            if is_lt:
                for q in range(r.shape[0]):
                    a_ref[:, off:off + LANE] = r[q]
                    off += LANE
            else:
                a_ref[:, off:off + r.shape[1]] = r[...]
                off += r.shape[1]

    acc = _dot(a_ref[...], w_ref[...])
    o_ref[0] = h_ref[0] + gate_ref[0] * acc


def _outproj(lhs, lt_flags, w, h, gate, *, tm=1024, tn=512):
    nb, seq, d = h.shape
    tm = min(tm, seq)
    rpb = seq // tm
    ktot = w.shape[0]
    in_specs = []
    for a, is_lt in zip(lhs, lt_flags):
        if is_lt:
            in_specs.append(pl.BlockSpec((a.shape[0], tm, LANE), lambda b, i, j: (0, b * rpb + i, 0)))
        else:
            in_specs.append(pl.BlockSpec((tm, a.shape[1]), lambda b, i, j: (b * rpb + i, 0)))
    in_specs += [
        pl.BlockSpec((ktot, tn), lambda b, i, j: (0, j)),
        pl.BlockSpec((1, tm, tn), lambda b, i, j: (b, i, j)),
        pl.BlockSpec((1, 1, tn), lambda b, i, j: (b, 0, j)),
    ]
    return pl.pallas_call(
        functools.partial(_outproj_kernel, lt_flags=tuple(lt_flags)),
        out_shape=jax.ShapeDtypeStruct((nb, seq, d), F32),
        grid=(nb, rpb, d // tn),
        in_specs=in_specs,
        out_specs=pl.BlockSpec((1, tm, tn), lambda b, i, j: (b, i, j)),
        scratch_shapes=[pltpu.VMEM((tm, ktot), BF16)],
        compiler_params=_cp(("parallel", "parallel", "arbitrary")),
        name="outproj",
    )(*lhs, w, h, gate[:, None, :])


def _glu_kernel(y_ref, w_ref, b_ref, o_ref, a_ref):
    j = pl.program_id(1)
    nq = o_ref.shape[1] // LANE

    @pl.when(j == 0)
    def _():
        for q in range(y_ref.shape[0]):
            a_ref[:, q * LANE:(q + 1) * LANE] = _gelu_tanh(y_ref[q].astype(F32)).astype(BF16)

    gate = _dot(a_ref[...], w_ref[...]) + b_ref[...]
    sg = _sigmoid(gate)
    for q in range(nq):
        y = y_ref[j * nq + q].astype(F32)
        o_ref[:, q * LANE:(q + 1) * LANE] = (y * sg[:, q * LANE:(q + 1) * LANE]).astype(o_ref.dtype)


def _glu(y_lt, w, b, *, tm=1024, tn=512):
    nt, t, _ = y_lt.shape
    n = w.shape[1]
    tm = min(tm, t)
    return pl.pallas_call(
        _glu_kernel,
        out_shape=jax.ShapeDtypeStruct((t, n), BF16),
        grid=(t // tm, n // tn),
        in_specs=[
            pl.BlockSpec((nt, tm, LANE), lambda i, j: (0, i, 0)),
            pl.BlockSpec((nt * LANE, tn), lambda i, j: (0, j)),
            pl.BlockSpec((1, tn), lambda i, j: (0, j)),
        ],
        out_specs=pl.BlockSpec((tm, tn), lambda i, j: (i, j)),
        scratch_shapes=[pltpu.VMEM((tm, nt * LANE), BF16)],
        compiler_params=_cp(("parallel", "arbitrary")),
        name="s5_glu",
    )(y_lt, w, b[None, :])


def _ssd_kernel(z_ref, x_ref, bm_ref, cm_ref, dt_ref, cw_ref, cb_ref, dtb_ref, ah_ref, dsk_ref, ng_ref,
                exp_ref, o_ref, ext_ref, st_ref, y_ref):
    c = pl.program_id(1)
    tc = x_ref.shape[0]
    cdim = ext_ref.shape[1]
    halo = 8

    @pl.when(c == 0)
    def _():
        ext_ref[0:halo, :] = jnp.zeros((halo, cdim), F32)
        st_ref[...] = jnp.zeros(st_ref.shape, F32)

    ext_ref[halo:halo + tc, 0:A_INNER] = x_ref[...].astype(F32)
    ext_ref[halo:halo + tc, A_INNER:A_INNER + A_GROUPS * A_STATE] = bm_ref[...].astype(F32)
    ext_ref[halo:halo + tc, A_INNER + A_GROUPS * A_STATE:cdim] = cm_ref[...].astype(F32)
    conv = cb_ref[...] + cw_ref[0:1, :] * ext_ref[halo - 3:halo - 3 + tc, :]
    for k in range(1, A_CONV_K):
        conv = conv + cw_ref[k:k + 1, :] * ext_ref[halo - 3 + k:halo - 3 + k + tc, :]
    tail = ext_ref[tc:tc + halo, :]
    ext_ref[0:halo, :] = tail
    xbc = _silu(conv)
    xs = xbc[:, 0:A_INNER]

    dt = _softplus(dt_ref[:, 0:A_HEADS] + dtb_ref[...])
    a = dt * ah_ref[...]
    row = lax.broadcasted_iota(I32, (tc, tc), 0)
    col = lax.broadcasted_iota(I32, (tc, tc), 1)
    tri = row >= col
    nh = A_HEADS
    c3 = _dot(jnp.where(tri, 1.0, 0.0).astype(BF16), _split3(a))
    acum = c3[:, 0:nh] + c3[:, nh:2 * nh] + c3[:, 2 * nh:3 * nh]
    acum_t = acum.T
    total = acum[tc - 1:tc, :]
    expand = exp_ref[...]
    dt_x = _dot(_split3(dt), expand)
    dec_in = _dot(_split3(jnp.exp(acum)), expand)
    dec_out = _dot(_split3(jnp.exp(total - acum)), expand)
    dec_tot = _dot(_split3(jnp.exp(total)), expand)
    xdt = xs * dt_x
    xdt_b = xdt.astype(BF16)
    xend_b = (xdt * dec_out).astype(BF16)

    for g in range(A_GROUPS):
        bg = xbc[:, A_INNER + g * A_STATE:A_INNER + (g + 1) * A_STATE].astype(BF16)
        cg = xbc[:, A_INNER + A_GROUPS * A_STATE + g * A_STATE:
                 A_INNER + A_GROUPS * A_STATE + (g + 1) * A_STATE].astype(BF16)
        cb = _dot_nt(cg, bg)
        lo, hi = g * A_GW, (g + 1) * A_GW
        st = st_ref[g]
        y_off = _dot(cg, st.astype(BF16)) * dec_in[:, lo:hi]
        st_ref[g] = st * dec_tot[:, lo:hi] + _dot(bg.T, xend_b[:, lo:hi])
        y_ref[:, lo:hi] = y_off
        for jh in range(A_HPG):
            hh = g * A_HPG + jh
            d = acum[:, hh:hh + 1] - acum_t[hh:hh + 1, :]
            m = (cb * jnp.exp(jnp.where(tri, d, NEG))).astype(BF16)
            f0 = hh * A_HEAD_DIM
            y_ref[:, f0:f0 + A_HEAD_DIM] = y_ref[:, f0:f0 + A_HEAD_DIM] + _dot(m, xdt_b[:, f0:f0 + A_HEAD_DIM])

    y = y_ref[...] + dsk_ref[...] * xs
    v = y * _silu(z_ref[...].astype(F32))
    for g in range(A_GROUPS):
        lo, hi = g * A_GW, (g + 1) * A_GW
        vg = v[:, lo:hi]
        ms = jnp.mean(vg * vg, axis=-1, keepdims=True)
        o_ref[:, lo:hi] = (vg * lax.rsqrt(ms + 1e-5) * ng_ref[:, lo:hi]).astype(o_ref.dtype)


def _ssd(zxbc, dt_raw, conv_w, conv_b, a_log, dt_bias, d_skip, norm_g, nb, seq, *, tc=128):
    t = nb * seq
    tc = min(tc, seq)
    cpb = seq // tc
    cdim = A_INNER + 2 * A_GROUPS * A_STATE
    nbc = A_GROUPS * A_STATE
    expand = (jnp.arange(A_INNER)[None, :] // A_HEAD_DIM == jnp.arange(3 * A_HEADS)[:, None] % A_HEADS).astype(BF16)
    a_head = -jnp.exp(a_log.astype(F32))[None, :]
    dsk = jnp.repeat(d_skip.astype(F32), A_HEAD_DIM)[None, :]
    const = lambda b, c: (0, 0)
    return pl.pallas_call(
        _ssd_kernel,
        out_shape=jax.ShapeDtypeStruct((t, A_INNER), BF16),
        grid=(nb, cpb),
        in_specs=[
            pl.BlockSpec((tc, A_INNER), lambda b, c: (b * cpb + c, 0)),
            pl.BlockSpec((tc, A_INNER), lambda b, c: (b * cpb + c, 1)),
            pl.BlockSpec((tc, nbc), lambda b, c: (b * cpb + c, 2 * A_INNER // nbc)),
            pl.BlockSpec((tc, nbc), lambda b, c: (b * cpb + c, 2 * A_INNER // nbc + 1)),
            pl.BlockSpec((tc, LANE), lambda b, c: (b * cpb + c, 0)),
            pl.BlockSpec((A_CONV_K, cdim), const),
            pl.BlockSpec((1, cdim), const),
            pl.BlockSpec((1, A_HEADS), const),
            pl.BlockSpec((1, A_HEADS), const),
            pl.BlockSpec((1, A_INNER), const),
            pl.BlockSpec((1, A_INNER), const),
            pl.BlockSpec((3 * A_HEADS, A_INNER), const),
        ],
        out_specs=pl.BlockSpec((tc, A_INNER), lambda b, c: (b * cpb + c, 0)),
        scratch_shapes=[
            pltpu.VMEM((tc + 8, cdim), F32),
            pltpu.VMEM((A_GROUPS, A_STATE, A_GW), F32),
            pltpu.VMEM((tc, A_INNER), F32),
        ],
        compiler_params=_cp(("parallel", "arbitrary")),
        name="ssd",
    )(zxbc, zxbc, zxbc, zxbc, dt_raw, conv_w.astype(F32), conv_b.astype(F32)[None, :],
      dt_bias.astype(F32)[None, :], a_head, dsk, norm_g.astype(F32)[None, :], expand)


def _s5_prep(lam_re, lam_im, log_step, b_re, b_im, c_re, c_im):
    hp = lax.Precision.HIGHEST
    lr, li = lam_re.astype(F32), lam_im.astype(F32)
    step = jnp.exp(log_step.astype(F32))
    mag = jnp.exp(lr * step)
    ang = li * step
    lb_re, lb_im = mag * jnp.cos(ang), mag * jnp.sin(ang)
    den = lr * lr + li * li
    g_re = ((lb_re - 1) * lr + lb_im * li) / den
    g_im = (lb_im * lr - (lb_re - 1) * li) / den
    br, bi = b_re.astype(F32), b_im.astype(F32)
    bb_re = g_re[..., None] * br - g_im[..., None] * bi
    bb_im = g_re[..., None] * bi + g_im[..., None] * br
    cr, ci = c_re.astype(F32), c_im.astype(F32)

    def lam_pow(n):
        n = jnp.asarray(n, F32)
        m = jnp.exp(n[..., None, None] * (lr * step))
        a = n[..., None, None] * ang
        return m * jnp.cos(a), m * jnp.sin(a)

    t = S5_T
    nt = S5_NT
    p_re, p_im = lam_pow(jnp.arange(t + 1))
    crt, cit = jnp.transpose(cr, (0, 2, 1)), jnp.transpose(ci, (0, 2, 1))
    cl_re = crt[None] * p_re[..., None] - cit[None] * p_im[..., None]
    cl_im = crt[None] * p_im[..., None] + cit[None] * p_re[..., None]
    kern = (jnp.einsum("dgpk,gph->dghk", cl_re[:t], bb_re, precision=hp)
            - jnp.einsum("dgpk,gph->dghk", cl_im[:t], bb_im, precision=hp))
    kd = kern.reshape(t, nt, LANE, B_GROUP)
    e_re, e_im = p_re[t - 1 - jnp.arange(t)], p_im[t - 1 - jnp.arange(t)]
    bbt_re, bbt_im = jnp.transpose(bb_re, (0, 2, 1)), jnp.transpose(bb_im, (0, 2, 1))
    ws_re = e_re[:, :, None, :] * bbt_re[None] - e_im[:, :, None, :] * bbt_im[None]
    ws_im = e_re[:, :, None, :] * bbt_im[None] + e_im[:, :, None, :] * bbt_re[None]
    wsc = jnp.stack([ws_re, ws_im], axis=0).reshape(2, t, nt, LANE, B_STATE)
    woc = jnp.stack([cl_re[1:t + 1], -cl_im[1:t + 1]], axis=0).reshape(2, t, nt, S5_GPT * B_STATE, B_GROUP)
    w_in, w_out = _s5_expand(kd.astype(BF16), wsc.astype(BF16), woc.astype(BF16))
    return w_in, w_out, (lr * step, ang)


def _s5_expand_kernel(kd_ref, ws_ref, wo_ref, win_ref, wout_ref):
    t = S5_T
    half = S5_GPT * B_STATE

    def expander(n_in, n_out):
        r = lax.broadcasted_iota(I32, (n_in, n_out), 0)
        c = lax.broadcasted_iota(I32, (n_in, n_out), 1)
        return jnp.where(c % n_in == r, 1.0, 0.0).astype(BF16)

    def same_group(rows, cols, rdiv, cdiv):
        r = lax.broadcasted_iota(I32, (rows, cols), 0)
        c = lax.broadcasted_iota(I32, (rows, cols), 1)
        return r // rdiv == c // cdiv

    e_k = expander(B_GROUP, LANE)
    e_p = expander(B_STATE, half)
    m_kk = same_group(LANE, LANE, B_GROUP, B_GROUP)
    m_kp = same_group(LANE, half, B_GROUP, B_STATE)
    m_pk = same_group(half, LANE, B_STATE, B_GROUP)
    zero = jnp.zeros((LANE, LANE), BF16)
    blocks = [jnp.where(m_kk, _dot(kd_ref[d, 0], e_k), 0.0).astype(BF16) for d in range(t)]
    for s in range(t):
        for u in range(t):
            win_ref[0, s * LANE:(s + 1) * LANE, u * LANE:(u + 1) * LANE] = blocks[u - s] if u >= s else zero
        for c in range(2):
            w = jnp.where(m_kp, _dot(ws_ref[c, s, 0], e_p), 0.0)
            win_ref[0, s * LANE:(s + 1) * LANE, t * LANE + c * half:t * LANE + (c + 1) * half] = w.astype(BF16)
    for c in range(2):
        for u in range(t):
            w = jnp.where(m_pk, _dot(wo_ref[c, u, 0], e_k), 0.0)
            wout_ref[0, c * half:(c + 1) * half, u * LANE:(u + 1) * LANE] = w.astype(BF16)


def _s5_expand(kd, wsc, woc):
    t, nt = S5_T, S5_NT
    half = S5_GPT * B_STATE
    nin = t * LANE
    return pl.pallas_call(
        _s5_expand_kernel,
        out_shape=[jax.ShapeDtypeStruct((nt, nin, nin + 2 * half), BF16),
                   jax.ShapeDtypeStruct((nt, 2 * half, nin), BF16)],
        grid=(nt,),
        in_specs=[pl.BlockSpec((t, 1, LANE, B_GROUP), lambda j: (0, j, 0, 0)),
                  pl.BlockSpec((2, t, 1, LANE, B_STATE), lambda j: (0, 0, j, 0, 0)),
                  pl.BlockSpec((2, t, 1, half, B_GROUP), lambda j: (0, 0, j, 0, 0))],
        out_specs=[pl.BlockSpec((1, nin, nin + 2 * half), lambda j: (j, 0, 0)),
                   pl.BlockSpec((1, 2 * half, nin), lambda j: (j, 0, 0))],
        compiler_params=_cp(("parallel",)),
        name="s5_expand",
    )(kd, wsc, woc)


def _s5_apow(lam_log, n_steps):
    lrs, ang = lam_log
    n = (S5_T * (2 ** jnp.arange(n_steps))).astype(F32)[:, None, None]
    m = jnp.exp(n * lrs)
    ap = jnp.stack([m * jnp.cos(n * ang), m * jnp.sin(n * ang)], axis=1)
    ap = ap.reshape(n_steps, 2, S5_NT, S5_GPT * B_STATE)
    return jnp.transpose(ap, (2, 0, 1, 3)).reshape(S5_NT, n_steps * 2, S5_GPT * B_STATE)


def _s5_kernel(x_ref, win_ref, wout_ref, ap_ref, dsk_ref, o_ref, *, n_steps):
    x = x_ref[0]
    r = x.shape[0]
    nin = S5_T * LANE
    half = S5_GPT * B_STATE
    ye = _dot(x, win_ref[0])
    s_re = ye[:, nin:nin + half]
    s_im = ye[:, nin + half:nin + 2 * half]
    rows = lax.broadcasted_iota(I32, (r, half), 0)
    for k in range(n_steps):
        d = 1 << k
        if d >= r:
            break
        a_re = ap_ref[0, 2 * k:2 * k + 1, :]
        a_im = ap_ref[0, 2 * k + 1:2 * k + 2, :]
        keep = rows >= d
        p_re = jnp.where(keep, pltpu.roll(s_re, d, 0), 0.0)
        p_im = jnp.where(keep, pltpu.roll(s_im, d, 0), 0.0)
        s_re, s_im = s_re + a_re * p_re - a_im * p_im, s_im + a_re * p_im + a_im * p_re
    keep = rows >= 1
    sp_re = jnp.where(keep, pltpu.roll(s_re, 1, 0), 0.0)
    sp_im = jnp.where(keep, pltpu.roll(s_im, 1, 0), 0.0)
    y = (ye[:, 0:nin] + _dot(sp_re.astype(BF16), wout_ref[0, 0:half, :])
         + _dot(sp_im.astype(BF16), wout_ref[0, half:2 * half, :]))
    o_ref[0] = (y + dsk_ref[0] * x.astype(F32)).astype(o_ref.dtype)


def _s5(u_lt, w_in, w_out, lam_log, d_skip, nb, seq):
    nt, t, _ = u_lt.shape
    r = seq // S5_T
    n_steps = max(1, (r - 1).bit_length())
    apow = _s5_apow(lam_log, n_steps)
    x = u_lt.reshape(nt, t // S5_T, S5_T * LANE)
    dsk = jnp.tile(d_skip.astype(F32).reshape(nt, 1, LANE), (1, 1, S5_T))
    nin = S5_T * LANE
    half = S5_GPT * B_STATE
    y = pl.pallas_call(
        functools.partial(_s5_kernel, n_steps=n_steps),
        out_shape=jax.ShapeDtypeStruct(x.shape, BF16),
        grid=(nt, nb),
        in_specs=[
            pl.BlockSpec((1, r, nin), lambda j, b: (j, b, 0)),
            pl.BlockSpec((1, nin, nin + 2 * half), lambda j, b: (j, 0, 0)),
            pl.BlockSpec((1, 2 * half, nin), lambda j, b: (j, 0, 0)),
            pl.BlockSpec((1, 2 * n_steps, half), lambda j, b: (j, 0, 0)),
            pl.BlockSpec((1, 1, nin), lambda j, b: (j, 0, 0)),
        ],
        out_specs=pl.BlockSpec((1, r, nin), lambda j, b: (j, b, 0)),
        compiler_params=_cp(("parallel", "parallel")),
        name="s5",
    )(x, w_in, w_out, apow, dsk)
    return y.reshape(nt, t, LANE)


def _hgrn_levels():
    nsub = C_CHUNK // C_SUB
    levels = []
    bs = 2
    while bs <= nsub:
        levels.append((bs, [(i // bs) * bs + bs // 2 - 1 for i in range(nsub)]))
        bs *= 2
    return levels


def _hgrn_kernel(q_ref, f_ref, i_ref, og_ref, lb_ref, ng_ref, o_ref, st_ref):
    cpb = q_ref.shape[1] // C_CHUNK
    nsub = C_CHUNK // C_SUB
    dh = C_HEAD_DIM

    @pl.when(pl.program_id(2) == 0)
    def _():
        st_ref[...] = jnp.zeros(st_ref.shape, F32)

    ng = ng_ref[...]
    row = lax.broadcasted_iota(I32, (C_CHUNK, dh), 0)
    rsub = row % C_SUB
    sub_of_row = row // C_SUB
    r2 = lax.broadcasted_iota(I32, (C_CHUNK, C_CHUNK), 0)
    c2 = lax.broadcasted_iota(I32, (C_CHUNK, C_CHUNK), 1)
    mask0 = (r2 // C_SUB == c2 // C_SUB) & (r2 >= c2)
    levels = _hgrn_levels()

    def chunk(ci, carry):
        for hh in range(q_ref.shape[0]):
            head_chunk(ci, hh)
        return carry

    def head_chunk(ci, hh):
        sl = pl.ds(pl.multiple_of(ci * C_CHUNK, C_CHUNK), C_CHUNK)
        lb = lb_ref[hh]
        q = _silu(q_ref[hh, sl, :].astype(F32))
        forget = lb + (1.0 - lb) * _sigmoid(f_ref[hh, sl, :].astype(F32))
        k = 1.0 - forget
        v = i_ref[hh, sl, :].astype(F32)
        g = jnp.log(forget)
        loc = g
        d = 1
        while d < C_SUB:
            loc = loc + jnp.where(rsub >= d, pltpu.roll(loc, d, 0), 0.0)
            d *= 2
        loc3 = loc.reshape(nsub, C_SUB, dh)
        last = [loc3[s, C_SUB - 1:C_SUB, :] for s in range(nsub)]
        pre = [jnp.zeros((1, dh), F32)]
        for s in range(nsub - 1):
            pre.append(pre[s] + last[s])
        ends = [pre[s] + last[s] for s in range(nsub)]
        bc = (loc3 + jnp.stack(pre, axis=0)).reshape(C_CHUNK, dh)
        total = ends[nsub - 1]

        def bcast_rows(vals):
            return jnp.broadcast_to(jnp.stack(vals, axis=0), (nsub, C_SUB, dh)).reshape(C_CHUNK, dh)

        ref0 = bcast_rows(pre)
        qd = (q * jnp.exp(bc - ref0)).astype(BF16)
        kd = (k * jnp.exp(ref0 - bc)).astype(BF16)
        scores = jnp.where(mask0, _dot_nt(qd, kd), 0.0)
        for bs, ref_sub in levels:
            ref = bcast_rows([ends[s] for s in ref_sub])
            upper = (sub_of_row % bs) >= (bs // 2)
            qd = (q * jnp.exp(jnp.where(upper, bc - ref, NEG))).astype(BF16)
            kd = (k * jnp.exp(jnp.where(upper, NEG, ref - bc))).astype(BF16)
            s_l = _dot_nt(qd, kd)
            if bs < nsub:
                s_l = jnp.where(r2 // (bs * C_SUB) == c2 // (bs * C_SUB), s_l, 0.0)
            scores = scores + s_l
        vb = v.astype(BF16)
        st_t = st_ref[hh]
        q_in = (q * jnp.exp(bc)).astype(BF16)
        o = _dot(scores.astype(BF16), vb) + _dot_nt(q_in, st_t.astype(BF16))
        k_end = (k * jnp.exp(total - bc)).astype(BF16)
        st_ref[hh] = st_t * jnp.exp(total) + _dot(v.T.astype(BF16), k_end)
        ms = jnp.mean(o * o, axis=-1, keepdims=True)
        out = o * lax.rsqrt(ms + RMS_EPS) * ng * _silu(og_ref[hh, sl, :].astype(F32))
        o_ref[hh, sl, :] = out.astype(o_ref.dtype)

    lax.fori_loop(0, cpb, chunk, 0)


def _hgrn(p_lt, lower_bound, norm_g, nb, seq, *, tl=1024, hps=2):
    nh = C_HEADS
    t = nb * seq
    tl = min(tl, seq)
    spb = seq // tl
    blk = lambda off: pl.BlockSpec((hps, tl, LANE), lambda b, h, s: (off // hps + h, b * spb + s, 0))
    return pl.pallas_call(
        _hgrn_kernel,
        out_shape=jax.ShapeDtypeStruct((nh, t, LANE), BF16),
        grid=(nb, nh // hps, spb),
        in_specs=[blk(0), blk(nh), blk(2 * nh), blk(3 * nh),
                  pl.BlockSpec((hps, 1, LANE), lambda b, h, s: (h, 0, 0)),
                  pl.BlockSpec((1, LANE), lambda b, h, s: (0, 0))],
        out_specs=pl.BlockSpec((hps, tl, LANE), lambda b, h, s: (h, b * spb + s, 0)),
        scratch_shapes=[pltpu.VMEM((hps, C_HEAD_DIM, C_HEAD_DIM), F32)],
        compiler_params=_cp(("parallel", "parallel", "arbitrary")),
        name="hgrn2",
    )(p_lt, p_lt, p_lt, p_lt, lower_bound.astype(F32).reshape(nh, 1, LANE), norm_g.astype(F32)[None, :])


def _pack_bf16_pair(lo, hi):
    lo_b = pltpu.bitcast(lo.astype(BF16).astype(F32), U32)
    hi_b = pltpu.bitcast(hi.astype(BF16).astype(F32), U32)
    return (hi_b & jnp.uint32(0xFFFF0000)) | (lo_b >> 16)


def _unpack_bf16_pair(u):
    lo = pltpu.bitcast(u << 16, F32)
    hi = pltpu.bitcast(u & jnp.uint32(0xFFFF0000), F32)
    return lo, hi


ROW_SUB = 8


def _store_rows_tiled(ref, v):
    n = v.shape[0]
    for s in range(ROW_SUB):
        ref[pl.ds(s, n, stride=ROW_SUB), :] = v[:, s * LANE:(s + 1) * LANE]


def _load_rows_tiled(ref, n):
    return [ref[pl.ds(s, n, stride=ROW_SUB), :] for s in range(ROW_SUB)]


def _router_kernel(h_ref, g_ref, sh_ref, sc_ref, rw_ref, rb_ref, xq_ref, ti_ref, tw_ref, rk_ref, cnt_ref, run_ref):
    @pl.when((pl.program_id(0) == 0) & (pl.program_id(1) == 0))
    def _():
        run_ref[...] = jnp.zeros(run_ref.shape, F32)

    hn = _norm_mod(h_ref[0], g_ref[...], sh_ref[0], sc_ref[0])
    half = hn.shape[1] // 2
    _store_rows_tiled(xq_ref, _pack_bf16_pair(hn[:, :half], hn[:, half:]))
    logits = _dot_hi(hn, rw_ref[...]) + rb_ref[...]
    lane = lax.broadcasted_iota(I32, logits.shape, 1)
    lane_f = lane.astype(F32)
    vals, idxs = [], []
    for _ in range(TOP_K):
        m = jnp.max(logits, axis=-1, keepdims=True)
        idx = jnp.min(jnp.where(logits == m, lane_f, float(LANE)), axis=-1, keepdims=True)
        vals.append(m)
        idxs.append(idx)
        logits = jnp.where(lane_f == idx, NEG, logits)
    exps = [jnp.exp(v - vals[0]) for v in vals]
    den = exps[0]
    for e in exps[1:]:
        den = den + e
    tm = logits.shape[0]
    hot = [lane_f == idxs[k] for k in range(TOP_K)]
    oh = jnp.where(hot[0] | hot[1] | hot[2] | hot[3], 1.0, 0.0)
    r2 = lax.broadcasted_iota(I32, (tm, tm), 0)
    c2 = lax.broadcasted_iota(I32, (tm, tm), 1)
    before = jnp.where(r2 > c2, 1.0, 0.0).astype(BF16)
    pref = _dot(before, oh.astype(BF16)) + run_ref[...]
    ti = jnp.zeros(lane.shape, F32)
    tw = jnp.zeros(lane.shape, F32)
    rk = jnp.zeros(lane.shape, F32)
    for k in range(TOP_K):
        ti = jnp.where(lane == k, idxs[k], ti)
        tw = jnp.where(lane == k, exps[k] / den, tw)
        rk = jnp.where(lane == k, jnp.sum(jnp.where(hot[k], pref, 0.0), axis=-1, keepdims=True), rk)
    ti_ref[...] = ti.astype(I32)
    tw_ref[...] = tw
    rk_ref[...] = rk.astype(I32)
    run_ref[...] = run_ref[...] + jnp.sum(oh, axis=0, keepdims=True)
    cnt_ref[...] = run_ref[...]


def _router(h, g, sh, sc, rw, rb, *, tm=512):
    nb, seq, d = h.shape
    tm = min(tm, seq)
    rpb = seq // tm
    t = nb * seq
    rwp = jnp.zeros((d, LANE), F32).at[:, :N_EXPERTS].set(rw.astype(F32))
    rbp = jnp.full((1, LANE), NEG, F32).at[0, :N_EXPERTS].set(rb.astype(F32))
    row = lambda b, i: (b * rpb + i, 0)
    return pl.pallas_call(
        _router_kernel,
        out_shape=[jax.ShapeDtypeStruct((t * ROW_SUB, LANE), U32),
                   jax.ShapeDtypeStruct((t, LANE), I32),
                   jax.ShapeDtypeStruct((t, LANE), F32),
                   jax.ShapeDtypeStruct((t, LANE), I32),
                   jax.ShapeDtypeStruct((1, LANE), F32)],
        grid=(nb, rpb),
        in_specs=[
            pl.BlockSpec((1, tm, d), lambda b, i: (b, i, 0)),
            pl.BlockSpec((1, d), lambda b, i: (0, 0)),
            pl.BlockSpec((1, 1, d), lambda b, i: (b, 0, 0)),
            pl.BlockSpec((1, 1, d), lambda b, i: (b, 0, 0)),
            pl.BlockSpec((d, LANE), lambda b, i: (0, 0)),
            pl.BlockSpec((1, LANE), lambda b, i: (0, 0)),
        ],
        out_specs=[pl.BlockSpec((tm * ROW_SUB, LANE), row), pl.BlockSpec((tm, LANE), row), pl.BlockSpec((tm, LANE), row),
                   pl.BlockSpec((tm, LANE), row), pl.BlockSpec((1, LANE), lambda b, i: (0, 0))],
        scratch_shapes=[pltpu.VMEM((1, LANE), F32)],
        compiler_params=_cp(("arbitrary", "arbitrary")),
        name="moe_router",
    )(h, g[None, :], sh[:, None, :], sc[:, None, :], rwp, rbp)


def _route_plan(top_i, rank, counts, tm):
    t = top_i.shape[0]
    na = t * TOP_K
    nt = na // tm + N_EXPERTS + 1
    ptiles = (counts + tm - 1) // tm
    tile_end = jnp.cumsum(ptiles)
    tile_start = tile_end - ptiles
    onehot = top_i[:, :, None] == jnp.arange(N_EXPERTS, dtype=I32)[None, None, :]
    start = jnp.sum(jnp.where(onehot, tile_start[None, None, :], 0), axis=-1)
    pos = (start * tm + rank).reshape(na)
    asg = jnp.full((nt * tm,), -1, I32).at[pos].set(jnp.arange(na, dtype=I32), unique_indices=True)
    tiles_done = jnp.sum((tile_end[None, :] <= jnp.arange(nt, dtype=I32)[:, None]).astype(I32), axis=1)
    tile_expert = jnp.minimum(tiles_done, N_EXPERTS - 1).astype(I32)
    n_valid = tile_end[-1:].astype(I32)
    return asg.reshape(nt, tm), tile_expert, n_valid


def _w1_prep_kernel(w_ref, p_ref, o_ref):
    nblk = o_ref.shape[1]
    wblk = o_ref.shape[3]
    for b in range(nblk):
        o_ref[0, b] = _dot(w_ref[0, :, b * wblk:(b + 1) * wblk].astype(BF16), p_ref[...]).astype(BF16)


def _w1_prep(w1, *, tr=512):
    e, d, n2 = w1.shape
    wblk = 2 * LANE
    nblk = n2 // wblk
    r = jnp.arange(wblk)
    perm = (jnp.where(r % 2 == 0, r // 2, LANE + r // 2)[:, None] == jnp.arange(wblk)[None, :]).astype(BF16)
    return pl.pallas_call(
        _w1_prep_kernel,
        out_shape=jax.ShapeDtypeStruct((e, nblk, d, wblk), BF16),
        grid=(e, d // tr),
        in_specs=[pl.BlockSpec((1, tr, n2), lambda x, i: (x, i, 0)),
                  pl.BlockSpec((wblk, wblk), lambda x, i: (0, 0))],
        out_specs=pl.BlockSpec((1, nblk, tr, wblk), lambda x, i: (x, 0, i, 0)),
        compiler_params=_cp(("parallel", "parallel")),
        name="moe_w1_prep",
    )(w1, perm)


def _experts_kernel(te_ref, nv_ref, asg_hbm, x_hbm, w1_ref, b1_ref, w2_ref, b2_ref, out_hbm,
                    asg_sm, xbuf, ybuf, sem_idx, sem_g, sem_s, *, n_tok):
    del te_ref
    i = pl.program_id(0)
    nt = pl.num_programs(0)
    nv = nv_ref[0]
    rs = ROW_SUB
    tm = xbuf.shape[1] // rs
    half = rs * LANE
    nblk = w1_ref.shape[1]
    na = TOP_K * n_tok

    def row_tile(row):
        return pl.ds(pl.multiple_of(row * rs, rs), rs)

    def dest_row(a, r):
        return jnp.where(a >= 0, (a & (TOP_K - 1)) * n_tok + (a >> 2), na + r)

    def asg_copy(tile, slot):
        return pltpu.make_async_copy(asg_hbm.at[tile], asg_sm.at[slot], sem_idx.at[slot])

    def gather_start(islot, xslot, r):
        tok = jnp.maximum(asg_sm[islot, r], 0) >> 2
        pltpu.make_async_copy(x_hbm.at[row_tile(tok)], xbuf.at[xslot, row_tile(r)], sem_g.at[xslot]).start()

    def scatter_start(islot, yslot, r):
        dst = dest_row(asg_sm[islot, r], r)
        pltpu.make_async_copy(ybuf.at[yslot, row_tile(r)], out_hbm.at[row_tile(dst)], sem_s.at[yslot]).start()

    def wait_gathers(xslot):
        pltpu.make_async_copy(x_hbm.at[pl.ds(0, tm * rs)], xbuf.at[xslot], sem_g.at[xslot]).wait()

    def wait_scatters(yslot):
        pltpu.make_async_copy(ybuf.at[yslot], out_hbm.at[pl.ds(0, tm * rs)], sem_s.at[yslot]).wait()

    @pl.when(i == 0)
    def _():
        ybuf[...] = jnp.zeros(ybuf.shape, U32)
        init = pltpu.make_async_copy(ybuf.at[0], out_hbm.at[pl.ds(na * rs, tm * rs)], sem_s.at[0])
        init.start()
        init.wait()

        def fill(r, c):
            asg_sm[3, r] = -1
            return c
        lax.fori_loop(0, tm, fill, 0)
        first = asg_copy(0, 0)
        first.start()
        first.wait()

        def body(r, c):
            gather_start(0, 0, r)
            return c
        lax.fori_loop(0, tm, body, 0)
        asg_copy(1, 1).start()

    @pl.when(i < nv)
    def _():
        slot = i % 2
        asg_copy(i + 1, (i + 1) % 4).wait()

        @pl.when(i + 2 <= nv)
        def _():
            asg_copy(i + 2, (i + 2) % 4).start()

        wait_gathers(slot)
        pieces = [_unpack_bf16_pair(p) for p in _load_rows_tiled(xbuf.at[slot], tm)]
        xb = jnp.concatenate([p[0].astype(BF16) for p in pieces] + [p[1].astype(BF16) for p in pieces], axis=1)
        g_slot, s_slot = (i + 1) % 4, (i + 3) % 4
        rows_per_blk = -(-tm // nblk)
        acts = []
        zmask = (nv >> 30).astype(U32)
        for b in range(nblk):
            for r in range(b * rows_per_blk, min((b + 1) * rows_per_blk, tm)):
                gather_start(g_slot, 1 - slot, r)
                scatter_start(s_slot, 1 - slot, r)
            probe = pltpu.bitcast(xbuf[slot, 0:1, :] & zmask, F32)
            gu = _dot(xb, w1_ref[0, b]) + b1_ref[0, b]
            gate = jnp.minimum(gu[:, 0:LANE] + probe, SWIGLU_LIMIT)
            up = jnp.clip(gu[:, LANE:2 * LANE], -SWIGLU_LIMIT, SWIGLU_LIMIT)
            acts.append(((up + 1.0) * (gate * _sigmoid(SWIGLU_ALPHA * gate))).astype(BF16))
        y = _dot(jnp.concatenate(acts, axis=1), w2_ref[0]) + b2_ref[0]

        @pl.when(i >= 1)
        def _():
            wait_scatters(slot)

        _store_rows_tiled(ybuf.at[slot], _pack_bf16_pair(y[:, :half], y[:, half:]))

    @pl.when(i == nv)
    def _():
        wait_gathers(nv % 2)
        yslot = (nv + 1) % 2
        islot = (nv + 3) % 4

        def body(r, c):
            scatter_start(islot, yslot, r)
            return c
        lax.fori_loop(0, tm, body, 0)
        wait_scatters(0)
        wait_scatters(1)


def _experts(xq, asg, tile_expert, n_valid, w1p, b1p, w2, b2, tm):
    t = xq.shape[0] // ROW_SUB
    nt = asg.shape[0]
    d = 2 * ROW_SUB * LANE
    nblk, wblk = w1p.shape[1], w1p.shape[3]
    e_map4 = lambda i, te, nv: (te[i], 0, 0, 0)
    grid_spec = pltpu.PrefetchScalarGridSpec(
        num_scalar_prefetch=2,
        grid=(nt,),
        in_specs=[
            pl.BlockSpec(memory_space=pl.ANY),
            pl.BlockSpec(memory_space=pl.ANY),
            pl.BlockSpec((1, nblk, d, wblk), e_map4),
            pl.BlockSpec((1, nblk, 1, wblk), e_map4),
            pl.BlockSpec((1, nblk * LANE, d), lambda i, te, nv: (te[i], 0, 0)),
            pl.BlockSpec((1, 1, d), lambda i, te, nv: (te[i], 0, 0)),
        ],
        out_specs=pl.BlockSpec(memory_space=pl.ANY),
        scratch_shapes=[
            pltpu.SMEM((4, tm), I32),
            pltpu.VMEM((2, tm * ROW_SUB, LANE), U32),
            pltpu.VMEM((2, tm * ROW_SUB, LANE), U32),
            pltpu.SemaphoreType.DMA((4,)),
            pltpu.SemaphoreType.DMA((2,)),
            pltpu.SemaphoreType.DMA((2,)),
        ],
    )
    return pl.pallas_call(
        functools.partial(_experts_kernel, n_tok=t),
        out_shape=jax.ShapeDtypeStruct(((TOP_K * t + tm) * ROW_SUB, LANE), U32),
        grid_spec=grid_spec,
        compiler_params=_cp(("arbitrary",)),
        name="moe_experts",
    )(tile_expert, n_valid, asg, xq, w1p, b1p, w2, b2)


def _combine_kernel(y0_ref, y1_ref, y2_ref, y3_ref, tw_ref, h_ref, gate_ref, o_ref):
    tm = tw_ref.shape[0]
    half = ROW_SUB * LANE
    ws = [tw_ref[:, k:k + 1] for k in range(TOP_K)]
    for s in range(ROW_SUB):
        acc_lo = acc_hi = None
        for k, y_ref in enumerate((y0_ref, y1_ref, y2_ref, y3_ref)):
            lo, hi = _unpack_bf16_pair(y_ref[pl.ds(s, tm, stride=ROW_SUB), :])
            acc_lo = lo * ws[k] if acc_lo is None else acc_lo + lo * ws[k]
            acc_hi = hi * ws[k] if acc_hi is None else acc_hi + hi * ws[k]
        c0, c1 = s * LANE, half + s * LANE
        o_ref[0, :, c0:c0 + LANE] = h_ref[0, :, c0:c0 + LANE] + gate_ref[0, :, c0:c0 + LANE] * acc_lo
        o_ref[0, :, c1:c1 + LANE] = h_ref[0, :, c1:c1 + LANE] + gate_ref[0, :, c1:c1 + LANE] * acc_hi


def _combine(y4, tw, h, gate, *, tm=512):
    nb, seq, d = h.shape
    tm = min(tm, seq)
    rpb = seq // tm
    t = nb * seq
    nblk = t // tm
    yspec = lambda k: pl.BlockSpec((tm * ROW_SUB, LANE), lambda b, i: (k * nblk + b * rpb + i, 0))
    return pl.pallas_call(
        _combine_kernel,
        out_shape=jax.ShapeDtypeStruct((nb, seq, d), F32),
        grid=(nb, rpb),
        in_specs=[yspec(0), yspec(1), yspec(2), yspec(3),
                  pl.BlockSpec((tm, LANE), lambda b, i: (b * rpb + i, 0)),
                  pl.BlockSpec((1, tm, d), lambda b, i: (b, i, 0)),
                  pl.BlockSpec((1, 1, d), lambda b, i: (b, 0, 0))],
        out_specs=pl.BlockSpec((1, tm, d), lambda b, i: (b, i, 0)),
        compiler_params=_cp(("parallel", "parallel")),
        name="moe_combine",
    )(y4, y4, y4, y4, tw, h, gate[:, None, :])


def _moe_layer(h, g, sh, sc, gate, rw, rb, w1, b1, w2, b2):
    xq, ti, tw, rk, cnt = _router(h, g, sh, sc, rw, rb)
    counts = cnt[0, :N_EXPERTS].astype(I32)
    asg, tile_expert, n_valid = _route_plan(ti[:, :TOP_K], rk[:, :TOP_K], counts, MOE_TM)
    ne, ff, d = w2.shape
    nblk = ff // LANE
    w1p = _w1_prep(w1)
    b1p = b1.astype(F32).reshape(ne, nblk, LANE, 2).transpose(0, 1, 3, 2).reshape(ne, nblk, 1, 2 * LANE)
    w2p = w2.astype(BF16)
    y4 = _experts(xq, asg, tile_expert, n_valid, w1p, b1p, w2p, b2[:, None, :].astype(F32), MOE_TM)
    return _combine(y4, tw, h, gate)


def _final_kernel(h_ref, g_ref, sh_ref, sc_ref, o_ref):
    o_ref[0] = _norm_mod(h_ref[0], g_ref[...], sh_ref[0], sc_ref[0])


def _final(h, g, sh, sc, *, tm=512):
    nb, seq, d = h.shape
    tm = min(tm, seq)
    return pl.pallas_call(
        _final_kernel,
        out_shape=jax.ShapeDtypeStruct((nb, seq, d), F32),
        grid=(nb, seq // tm),
        in_specs=[pl.BlockSpec((1, tm, d), lambda b, i: (b, i, 0)),
                  pl.BlockSpec((1, d), lambda b, i: (0, 0)),
                  pl.BlockSpec((1, 1, d), lambda b, i: (b, 0, 0)),
                  pl.BlockSpec((1, 1, d), lambda b, i: (b, 0, 0))],
        out_specs=pl.BlockSpec((1, tm, d), lambda b, i: (b, i, 0)),
        compiler_params=_cp(("parallel", "parallel")),
        name="final_norm",
    )(h, g[None, :], sh[:, None, :], sc[:, None, :])


def _ab_mixer(h, g, sh, sc, gate, w_in, conv_w, conv_b, a_log, dt_bias, ssd_d, ssd_norm_g,
              lam_re, lam_im, log_step, b_re, b_im, c_re, c_im, s5_d, glu_w, glu_b, w_out):
    nb, seq, d = h.shape
    n_zxbc = A_INNER + A_INNER + 2 * A_GROUPS * A_STATE
    w_zxbc = w_in[:, :n_zxbc].astype(BF16)
    tail_t = lax.optimization_barrier(w_in[:, n_zxbc:].astype(BF16).T)
    w_dt = jnp.pad(tail_t[:A_HEADS], ((0, LANE - A_HEADS), (0, 0))).T
    w_u = lax.optimization_barrier(tail_t[A_HEADS:]).T
    zxbc, dt_raw = _proj(h, g, sh, sc, w_zxbc, out_lt=False, w_extra=w_dt)
    u_lt = _proj(h, g, sh, sc, w_u, out_lt=True)
    y_a = _ssd(zxbc, dt_raw, conv_w, conv_b, a_log, dt_bias, ssd_d, ssd_norm_g, nb, seq)
    s5_in, s5_out, lam_log = _s5_prep(lam_re, lam_im, log_step, b_re, b_im, c_re, c_im)
    y_b = _s5(u_lt, s5_in, s5_out, lam_log, s5_d, nb, seq)
    y_b = _glu(y_b, glu_w.astype(BF16), glu_b.astype(F32))
    return _outproj([y_a, y_b], [False, False], w_out.astype(BF16), h, gate)


def _hgrn_mixer(h, g, sh, sc, gate, w_in, lower_bound, norm_g, w_out):
    nb, seq, _ = h.shape
    p_lt = _proj(h, g, sh, sc, w_in.astype(BF16), out_lt=True)
    o_lt = _hgrn(p_lt, lower_bound, norm_g, nb, seq)
    return _outproj([o_lt], [True], w_out.astype(BF16), h, gate)


def kernel(x, c, ada_w, ada_b, norm1_g, norm2_g, ab_w_in, ab_conv_w, ab_conv_b, ssd_a_log, ssd_dt_bias, ssd_d, ssd_norm_g, s5_lam_re, s5_lam_im, s5_log_step, s5_b_re, s5_b_im, s5_c_re, s5_c_im, s5_d, s5_glu_w, s5_glu_b, ab_w_out, hg_w_in, hg_lower_bounds, hg_norm_g, hg_w_out, moe_router_w, moe_router_b, moe_w1, moe_b1, moe_w2, moe_b2, final_ada_w, final_ada_b, final_norm_g):
    depth = ada_w.shape[0]
    d = x.shape[-1]
    mods = _ada(c, ada_w, ada_b)
    fmod = _ada(c, final_ada_w[None], final_ada_b[None])[0]
    lb_soft = jax.nn.softmax(hg_lower_bounds.astype(F32), axis=0)
    lbs = jnp.cumsum(lb_soft, axis=0) - lb_soft[0]
    h = x
    for l in range(depth):
        sh1, sc1, g1, sh2, sc2, g2 = [mods[l][:, k * d:(k + 1) * d] for k in range(6)]
        i = l // 2
        if l % 2 == 0:
            h = _ab_mixer(h, norm1_g[l], sh1, sc1, g1, ab_w_in[i], ab_conv_w[i], ab_conv_b[i], ssd_a_log[i],
                          ssd_dt_bias[i], ssd_d[i], ssd_norm_g[i], s5_lam_re[i], s5_lam_im[i], s5_log_step[i],
                          s5_b_re[i], s5_b_im[i], s5_c_re[i], s5_c_im[i], s5_d[i], s5_glu_w[i], s5_glu_b[i],
                          ab_w_out[i])
        else:
            h = _hgrn_mixer(h, norm1_g[l], sh1, sc1, g1, hg_w_in[i], lbs[l], hg_norm_g[i], hg_w_out[i])
        h = _moe_layer(h, norm2_g[l], sh2, sc2, g2, moe_router_w[l], moe_router_b[l], moe_w1[l], moe_b1[l],
                       moe_w2[l], moe_b2[l])
    return _final(h, final_norm_g, fmod[:, :d], fmod[:, d:])
```

```python
import functools
import math

import jax
import jax.numpy as jnp
from jax import lax
from jax.experimental import pallas as pl
from jax.experimental.pallas import tpu as pltpu

F32 = jnp.float32
BF16 = jnp.bfloat16
U32 = jnp.uint32
I32 = jnp.int32

LANE = 128
VMEM_LIMIT = 56 * 1024 * 1024

D_MODEL = 2048
RMS_EPS = 1e-6

A_HEAD_DIM = 64
A_INNER = D_MODEL
A_HEADS = A_INNER // A_HEAD_DIM
A_GROUPS = 4
A_HPG = A_HEADS // A_GROUPS
A_STATE = 128
A_CONV_K = 4
A_GW = A_INNER // A_GROUPS

B_WIDTH = D_MODEL
B_GROUP = 16
B_GROUPS = B_WIDTH // B_GROUP
B_STATE = 64
S5_T = 16
S5_GPT = LANE // B_GROUP
S5_NT = B_WIDTH // LANE

C_HEAD_DIM = 128
C_WIDTH = D_MODEL
C_HEADS = C_WIDTH // C_HEAD_DIM
C_SUB = 16
C_CHUNK = 128

N_EXPERTS = 32
TOP_K = 4
EXPERT_FF = 3 * D_MODEL // 8
SWIGLU_LIMIT = 7.0
SWIGLU_ALPHA = 1.702
MOE_TM = 256

NEG = -1e30


def _cp(sem):
    return pltpu.CompilerParams(dimension_semantics=sem, vmem_limit_bytes=VMEM_LIMIT)


def _sigmoid(x):
    return 1.0 / (1.0 + jnp.exp(-x))


def _silu(x):
    return x * _sigmoid(x)


def _softplus(x):
    return jnp.maximum(x, 0.0) + jnp.log(1.0 + jnp.exp(-jnp.abs(x)))


def _gelu_tanh(x):
    return 0.5 * x * (1.0 + jnp.tanh(math.sqrt(2.0 / math.pi) * (x + 0.044715 * x * x * x)))


def _norm_mod(h, g, sh, sc):
    ms = jnp.mean(h * h, axis=-1, keepdims=True)
    return h * lax.rsqrt(ms + RMS_EPS) * g * (1.0 + sc) + sh


def _dot(a, b):
    return jnp.dot(a, b, preferred_element_type=F32)


def _dot_nt(a, b):
    return lax.dot_general(a, b, (((1,), (1,)), ((), ())), preferred_element_type=F32)


def _split3(x):
    p1 = x.astype(BF16)
    r1 = x - p1.astype(F32)
    p2 = r1.astype(BF16)
    p3 = (r1 - p2.astype(F32)).astype(BF16)
    return jnp.concatenate([p1, p2, p3], axis=1)


def _dot_hi(a, b):
    return jnp.dot(a, b, preferred_element_type=F32, precision=lax.Precision.HIGHEST)


def _ada_kernel(csb_ref, w_ref, b_ref, o_ref):
    nb = csb_ref.shape[0]
    tn = w_ref.shape[2]
    for q in range(tn // LANE):
        w = w_ref[0, :, q * LANE:(q + 1) * LANE]
        for b in range(nb):
            s = jnp.sum(w * csb_ref[b], axis=0, keepdims=True)
            o_ref[0, b:b + 1, q * LANE:(q + 1) * LANE] = s + b_ref[0, :, q * LANE:(q + 1) * LANE]


def _ada(c, w, bias, tn=1024):
    nl, d, n = w.shape
    nb = c.shape[0]
    cs = c * jax.nn.sigmoid(c)
    csb = jnp.broadcast_to(cs[:, :, None], (nb, d, LANE))
    return pl.pallas_call(
        _ada_kernel,
        out_shape=jax.ShapeDtypeStruct((nl, nb, n), F32),
        grid=(nl, n // tn),
        in_specs=[
            pl.BlockSpec((nb, d, LANE), lambda l, j: (0, 0, 0)),
            pl.BlockSpec((1, d, tn), lambda l, j: (l, 0, j)),
            pl.BlockSpec((1, 1, tn), lambda l, j: (l, 0, j)),
        ],
        out_specs=pl.BlockSpec((1, nb, tn), lambda l, j: (l, 0, j)),
        compiler_params=_cp(("parallel", "parallel")),
        name="ada_mod",
    )(csb, w, bias[:, None, :])


def _proj_kernel(h_ref, g_ref, sh_ref, sc_ref, w_ref, *rest, out_lt, has_extra):
    if has_extra:
        wx_ref, o_ref, ox_ref, hn_ref = rest
    else:
        o_ref, hn_ref = rest
    j = pl.program_id(2)

    @pl.when(j == 0)
    def _():
        hn = _norm_mod(h_ref[0], g_ref[...], sh_ref[0], sc_ref[0])
        hn_ref[...] = hn.astype(BF16)
        if has_extra:
            ox_ref[...] = _dot(hn_ref[...], wx_ref[...])

    acc = _dot(hn_ref[...], w_ref[...])
    if out_lt:
        for q in range(o_ref.shape[0]):
            o_ref[q] = acc[:, q * LANE:(q + 1) * LANE].astype(o_ref.dtype)
    else:
        o_ref[...] = acc.astype(o_ref.dtype)


def _proj(h, g, sh, sc, w, *, out_lt, w_extra=None, tm=1024, tn=512):
    nb, seq, d = h.shape
    n = w.shape[1]
    tm = min(tm, seq)
    rpb = seq // tm
    t = nb * seq
    has_extra = w_extra is not None
    in_specs = [
        pl.BlockSpec((1, tm, d), lambda b, i, j: (b, i, 0)),
        pl.BlockSpec((1, d), lambda b, i, j: (0, 0)),
        pl.BlockSpec((1, 1, d), lambda b, i, j: (b, 0, 0)),
        pl.BlockSpec((1, 1, d), lambda b, i, j: (b, 0, 0)),
        pl.BlockSpec((d, tn), lambda b, i, j: (0, j)),
    ]
    args = [h, g[None, :], sh[:, None, :], sc[:, None, :], w]
    if out_lt:
        out_shape = [jax.ShapeDtypeStruct((n // LANE, t, LANE), BF16)]
        out_specs = [pl.BlockSpec((tn // LANE, tm, LANE), lambda b, i, j: (j, b * rpb + i, 0))]
    else:
        out_shape = [jax.ShapeDtypeStruct((t, n), BF16)]
        out_specs = [pl.BlockSpec((tm, tn), lambda b, i, j: (b * rpb + i, j))]
    if has_extra:
        in_specs.append(pl.BlockSpec((d, LANE), lambda b, i, j: (0, 0)))
        args.append(w_extra)
        out_shape.append(jax.ShapeDtypeStruct((t, LANE), F32))
        out_specs.append(pl.BlockSpec((tm, LANE), lambda b, i, j: (b * rpb + i, 0)))
    res = pl.pallas_call(
        functools.partial(_proj_kernel, out_lt=out_lt, has_extra=has_extra),
        out_shape=out_shape,
        grid=(nb, rpb, n // tn),
        in_specs=in_specs,
        out_specs=out_specs,
        scratch_shapes=[pltpu.VMEM((tm, d), BF16)],
        compiler_params=_cp(("parallel", "parallel", "arbitrary")),
        name="proj_lt" if out_lt else "proj_std",
    )(*args)
    return res if has_extra else res[0]


def _outproj_kernel(*refs, lt_flags):
    nl = len(lt_flags)
    lhs_refs = refs[:nl]
    w_ref, h_ref, gate_ref, o_ref, a_ref = refs[nl:]
    j = pl.program_id(2)

    @pl.when(j == 0)
    def _():
        off = 0
        for r, is_lt in zip(lhs_refs, lt_flags):
            if is_lt:
                for q in range(r.shape[0]):
                    a_ref[:, off:off + LANE] = r[q]
                    off += LANE
            else:
                a_ref[:, off:off + r.shape[1]] = r[...]
                off += r.shape[1]

    acc = _dot(a_ref[...], w_ref[...])
    o_ref[0] = h_ref[0] + gate_ref[0] * acc


def _outproj(lhs, lt_flags, w, h, gate, *, tm=1024, tn=512):
    nb, seq, d = h.shape
    tm = min(tm, seq)
    rpb = seq // tm
    ktot = w.shape[0]
    in_specs = []
    for a, is_lt in zip(lhs, lt_flags):
        if is_lt:
            in_specs.append(pl.BlockSpec((a.shape[0], tm, LANE), lambda b, i, j: (0, b * rpb + i, 0)))
        else:
            in_specs.append(pl.BlockSpec((tm, a.shape[1]), lambda b, i, j: (b * rpb + i, 0)))
    in_specs += [
        pl.BlockSpec((ktot, tn), lambda b, i, j: (0, j)),
        pl.BlockSpec((1, tm, tn), lambda b, i, j: (b, i, j)),
        pl.BlockSpec((1, 1, tn), lambda b, i, j: (b, 0, j)),
    ]
    return pl.pallas_call(
        functools.partial(_outproj_kernel, lt_flags=tuple(lt_flags)),
        out_shape=jax.ShapeDtypeStruct((nb, seq, d), F32),
        grid=(nb, rpb, d // tn),
        in_specs=in_specs,
        out_specs=pl.BlockSpec((1, tm, tn), lambda b, i, j: (b, i, j)),
        scratch_shapes=[pltpu.VMEM((tm, ktot), BF16)],
        compiler_params=_cp(("parallel", "parallel", "arbitrary")),
        name="outproj",
    )(*lhs, w, h, gate[:, None, :])


def _glu_kernel(y_ref, w_ref, b_ref, o_ref, a_ref):
    j = pl.program_id(1)
    nq = o_ref.shape[1] // LANE

    @pl.when(j == 0)
    def _():
        for q in range(y_ref.shape[0]):
            a_ref[:, q * LANE:(q + 1) * LANE] = _gelu_tanh(y_ref[q].astype(F32)).astype(BF16)

    gate = _dot(a_ref[...], w_ref[...]) + b_ref[...]
    sg = _sigmoid(gate)
    for q in range(nq):
        y = y_ref[j * nq + q].astype(F32)
        o_ref[:, q * LANE:(q + 1) * LANE] = (y * sg[:, q * LANE:(q + 1) * LANE]).astype(o_ref.dtype)


def _glu(y_lt, w, b, *, tm=1024, tn=512):
    nt, t, _ = y_lt.shape
    n = w.shape[1]
    tm = min(tm, t)
    return pl.pallas_call(
        _glu_kernel,
        out_shape=jax.ShapeDtypeStruct((t, n), BF16),
        grid=(t // tm, n // tn),
        in_specs=[
            pl.BlockSpec((nt, tm, LANE), lambda i, j: (0, i, 0)),
            pl.BlockSpec((nt * LANE, tn), lambda i, j: (0, j)),
            pl.BlockSpec((1, tn), lambda i, j: (0, j)),
        ],
        out_specs=pl.BlockSpec((tm, tn), lambda i, j: (i, j)),
        scratch_shapes=[pltpu.VMEM((tm, nt * LANE), BF16)],
        compiler_params=_cp(("parallel", "arbitrary")),
        name="s5_glu",
    )(y_lt, w, b[None, :])


def _ssd_kernel(z_ref, x_ref, bm_ref, cm_ref, dt_ref, cw_ref, cb_ref, dtb_ref, ah_ref, dsk_ref, ng_ref,
                exp_ref, o_ref, ext_ref, st_ref, y_ref):
    c = pl.program_id(1)
    tc = x_ref.shape[0]
    cdim = ext_ref.shape[1]
    halo = 8

    @pl.when(c == 0)
    def _():
        ext_ref[0:halo, :] = jnp.zeros((halo, cdim), F32)
        st_ref[...] = jnp.zeros(st_ref.shape, F32)

    ext_ref[halo:halo + tc, 0:A_INNER] = x_ref[...].astype(F32)
    ext_ref[halo:halo + tc, A_INNER:A_INNER + A_GROUPS * A_STATE] = bm_ref[...].astype(F32)
    ext_ref[halo:halo + tc, A_INNER + A_GROUPS * A_STATE:cdim] = cm_ref[...].astype(F32)
    conv = cb_ref[...] + cw_ref[0:1, :] * ext_ref[halo - 3:halo - 3 + tc, :]
    for k in range(1, A_CONV_K):
        conv = conv + cw_ref[k:k + 1, :] * ext_ref[halo - 3 + k:halo - 3 + k + tc, :]
    tail = ext_ref[tc:tc + halo, :]
    ext_ref[0:halo, :] = tail
    xbc = _silu(conv)
    xs = xbc[:, 0:A_INNER]

    dt = _softplus(dt_ref[:, 0:A_HEADS] + dtb_ref[...])
    a = dt * ah_ref[...]
    row = lax.broadcasted_iota(I32, (tc, tc), 0)
    col = lax.broadcasted_iota(I32, (tc, tc), 1)
    tri = row >= col
    nh = A_HEADS
    c3 = _dot(jnp.where(tri, 1.0, 0.0).astype(BF16), _split3(a))
    acum = c3[:, 0:nh] + c3[:, nh:2 * nh] + c3[:, 2 * nh:3 * nh]
    acum_t = acum.T
    total = acum[tc - 1:tc, :]
    expand = exp_ref[...]
    dt_x = _dot(_split3(dt), expand)
    dec_in = _dot(_split3(jnp.exp(acum)), expand)
    dec_out = _dot(_split3(jnp.exp(total - acum)), expand)
    dec_tot = _dot(_split3(jnp.exp(total)), expand)
    xdt = xs * dt_x
    xdt_b = xdt.astype(BF16)
    xend_b = (xdt * dec_out).astype(BF16)

    for g in range(A_GROUPS):
        bg = xbc[:, A_INNER + g * A_STATE:A_INNER + (g + 1) * A_STATE].astype(BF16)
        cg = xbc[:, A_INNER + A_GROUPS * A_STATE + g * A_STATE:
                 A_INNER + A_GROUPS * A_STATE + (g + 1) * A_STATE].astype(BF16)
        cb = _dot_nt(cg, bg)
        lo, hi = g * A_GW, (g + 1) * A_GW
        st = st_ref[g]
        y_off = _dot(cg, st.astype(BF16)) * dec_in[:, lo:hi]
        st_ref[g] = st * dec_tot[:, lo:hi] + _dot(bg.T, xend_b[:, lo:hi])
        y_ref[:, lo:hi] = y_off
        for jh in range(A_HPG):
            hh = g * A_HPG + jh
            d = acum[:, hh:hh + 1] - acum_t[hh:hh + 1, :]
            m = (cb * jnp.exp(jnp.where(tri, d, NEG))).astype(BF16)
            f0 = hh * A_HEAD_DIM
            y_ref[:, f0:f0 + A_HEAD_DIM] = y_ref[:, f0:f0 + A_HEAD_DIM] + _dot(m, xdt_b[:, f0:f0 + A_HEAD_DIM])

    y = y_ref[...] + dsk_ref[...] * xs
    v = y * _silu(z_ref[...].astype(F32))
    for g in range(A_GROUPS):
        lo, hi = g * A_GW, (g + 1) * A_GW
        vg = v[:, lo:hi]
        ms = jnp.mean(vg * vg, axis=-1, keepdims=True)
        o_ref[:, lo:hi] = (vg * lax.rsqrt(ms + 1e-5) * ng_ref[:, lo:hi]).astype(o_ref.dtype)


def _ssd(zxbc, dt_raw, conv_w, conv_b, a_log, dt_bias, d_skip, norm_g, nb, seq, *, tc=128):
    t = nb * seq
    tc = min(tc, seq)
    cpb = seq // tc
    cdim = A_INNER + 2 * A_GROUPS * A_STATE
    nbc = A_GROUPS * A_STATE
    expand = (jnp.arange(A_INNER)[None, :] // A_HEAD_DIM == jnp.arange(3 * A_HEADS)[:, None] % A_HEADS).astype(BF16)
    a_head = -jnp.exp(a_log.astype(F32))[None, :]
    dsk = jnp.repeat(d_skip.astype(F32), A_HEAD_DIM)[None, :]
    const = lambda b, c: (0, 0)
    return pl.pallas_call(
        _ssd_kernel,
        out_shape=jax.ShapeDtypeStruct((t, A_INNER), BF16),
        grid=(nb, cpb),
        in_specs=[
            pl.BlockSpec((tc, A_INNER), lambda b, c: (b * cpb + c, 0)),
            pl.BlockSpec((tc, A_INNER), lambda b, c: (b * cpb + c, 1)),
            pl.BlockSpec((tc, nbc), lambda b, c: (b * cpb + c, 2 * A_INNER // nbc)),
            pl.BlockSpec((tc, nbc), lambda b, c: (b * cpb + c, 2 * A_INNER // nbc + 1)),
            pl.BlockSpec((tc, LANE), lambda b, c: (b * cpb + c, 0)),
            pl.BlockSpec((A_CONV_K, cdim), const),
            pl.BlockSpec((1, cdim), const),
            pl.BlockSpec((1, A_HEADS), const),
            pl.BlockSpec((1, A_HEADS), const),
            pl.BlockSpec((1, A_INNER), const),
            pl.BlockSpec((1, A_INNER), const),
            pl.BlockSpec((3 * A_HEADS, A_INNER), const),
        ],
        out_specs=pl.BlockSpec((tc, A_INNER), lambda b, c: (b * cpb + c, 0)),
        scratch_shapes=[
            pltpu.VMEM((tc + 8, cdim), F32),
            pltpu.VMEM((A_GROUPS, A_STATE, A_GW), F32),
            pltpu.VMEM((tc, A_INNER), F32),
        ],
        compiler_params=_cp(("parallel", "arbitrary")),
        name="ssd",
    )(zxbc, zxbc, zxbc, zxbc, dt_raw, conv_w.astype(F32), conv_b.astype(F32)[None, :],
      dt_bias.astype(F32)[None, :], a_head, dsk, norm_g.astype(F32)[None, :], expand)


def _s5_prep(lam_re, lam_im, log_step, b_re, b_im, c_re, c_im):
    hp = lax.Precision.HIGHEST
    lr, li = lam_re.astype(F32), lam_im.astype(F32)
    step = jnp.exp(log_step.astype(F32))
    mag = jnp.exp(lr * step)
    ang = li * step
    lb_re, lb_im = mag * jnp.cos(ang), mag * jnp.sin(ang)
    den = lr * lr + li * li
    g_re = ((lb_re - 1) * lr + lb_im * li) / den
    g_im = (lb_im * lr - (lb_re - 1) * li) / den
    br, bi = b_re.astype(F32), b_im.astype(F32)
    bb_re = g_re[..., None] * br - g_im[..., None] * bi
    bb_im = g_re[..., None] * bi + g_im[..., None] * br
    cr, ci = c_re.astype(F32), c_im.astype(F32)

    def lam_pow(n):
        n = jnp.asarray(n, F32)
        m = jnp.exp(n[..., None, None] * (lr * step))
        a = n[..., None, None] * ang
        return m * jnp.cos(a), m * jnp.sin(a)

    t = S5_T
    nt = S5_NT
    p_re, p_im = lam_pow(jnp.arange(t + 1))
    crt, cit = jnp.transpose(cr, (0, 2, 1)), jnp.transpose(ci, (0, 2, 1))
    cl_re = crt[None] * p_re[..., None] - cit[None] * p_im[..., None]
    cl_im = crt[None] * p_im[..., None] + cit[None] * p_re[..., None]
    kern = (jnp.einsum("dgpk,gph->dghk", cl_re[:t], bb_re, precision=hp)
            - jnp.einsum("dgpk,gph->dghk", cl_im[:t], bb_im, precision=hp))
    kd = kern.reshape(t, nt, LANE, B_GROUP)
    e_re, e_im = p_re[t - 1 - jnp.arange(t)], p_im[t - 1 - jnp.arange(t)]
    bbt_re, bbt_im = jnp.transpose(bb_re, (0, 2, 1)), jnp.transpose(bb_im, (0, 2, 1))
    ws_re = e_re[:, :, None, :] * bbt_re[None] - e_im[:, :, None, :] * bbt_im[None]
    ws_im = e_re[:, :, None, :] * bbt_im[None] + e_im[:, :, None, :] * bbt_re[None]
    wsc = jnp.stack([ws_re, ws_im], axis=0).reshape(2, t, nt, LANE, B_STATE)
    woc = jnp.stack([cl_re[1:t + 1], -cl_im[1:t + 1]], axis=0).reshape(2, t, nt, S5_GPT * B_STATE, B_GROUP)
    w_in, w_out = _s5_expand(kd.astype(BF16), wsc.astype(BF16), woc.astype(BF16))
    return w_in, w_out, (lr * step, ang)


def _s5_expand_kernel(kd_ref, ws_ref, wo_ref, win_ref, wout_ref):
    t = S5_T
    half = S5_GPT * B_STATE

    def expander(n_in, n_out):
        r = lax.broadcasted_iota(I32, (n_in, n_out), 0)
        c = lax.broadcasted_iota(I32, (n_in, n_out), 1)
        return jnp.where(c % n_in == r, 1.0, 0.0).astype(BF16)

    def same_group(rows, cols, rdiv, cdiv):
        r = lax.broadcasted_iota(I32, (rows, cols), 0)
        c = lax.broadcasted_iota(I32, (rows, cols), 1)
        return r // rdiv == c // cdiv

    e_k = expander(B_GROUP, LANE)
    e_p = expander(B_STATE, half)
    m_kk = same_group(LANE, LANE, B_GROUP, B_GROUP)
    m_kp = same_group(LANE, half, B_GROUP, B_STATE)
    m_pk = same_group(half, LANE, B_STATE, B_GROUP)
    zero = jnp.zeros((LANE, LANE), BF16)
    blocks = [jnp.where(m_kk, _dot(kd_ref[d, 0], e_k), 0.0).astype(BF16) for d in range(t)]
    for s in range(t):
        for u in range(t):
            win_ref[0, s * LANE:(s + 1) * LANE, u * LANE:(u + 1) * LANE] = blocks[u - s] if u >= s else zero
        for c in range(2):
            w = jnp.where(m_kp, _dot(ws_ref[c, s, 0], e_p), 0.0)
            win_ref[0, s * LANE:(s + 1) * LANE, t * LANE + c * half:t * LANE + (c + 1) * half] = w.astype(BF16)
    for c in range(2):
        for u in range(t):
            w = jnp.where(m_pk, _dot(wo_ref[c, u, 0], e_k), 0.0)
            wout_ref[0, c * half:(c + 1) * half, u * LANE:(u + 1) * LANE] = w.astype(BF16)


def _s5_expand(kd, wsc, woc):
    t, nt = S5_T, S5_NT
    half = S5_GPT * B_STATE
    nin = t * LANE
    return pl.pallas_call(
        _s5_expand_kernel,
        out_shape=[jax.ShapeDtypeStruct((nt, nin, nin + 2 * half), BF16),
                   jax.ShapeDtypeStruct((nt, 2 * half, nin), BF16)],
        grid=(nt,),
        in_specs=[pl.BlockSpec((t, 1, LANE, B_GROUP), lambda j: (0, j, 0, 0)),
                  pl.BlockSpec((2, t, 1, LANE, B_STATE), lambda j: (0, 0, j, 0, 0)),
                  pl.BlockSpec((2, t, 1, half, B_GROUP), lambda j: (0, 0, j, 0, 0))],
        out_specs=[pl.BlockSpec((1, nin, nin + 2 * half), lambda j: (j, 0, 0)),
                   pl.BlockSpec((1, 2 * half, nin), lambda j: (j, 0, 0))],
        compiler_params=_cp(("parallel",)),
        name="s5_expand",
    )(kd, wsc, woc)


def _s5_apow(lam_log, n_steps):
    lrs, ang = lam_log
    n = (S5_T * (2 ** jnp.arange(n_steps))).astype(F32)[:, None, None]
    m = jnp.exp(n * lrs)
    ap = jnp.stack([m * jnp.cos(n * ang), m * jnp.sin(n * ang)], axis=1)
    ap = ap.reshape(n_steps, 2, S5_NT, S5_GPT * B_STATE)
    return jnp.transpose(ap, (2, 0, 1, 3)).reshape(S5_NT, n_steps * 2, S5_GPT * B_STATE)


def _s5_kernel(x_ref, win_ref, wout_ref, ap_ref, dsk_ref, o_ref, *, n_steps):
    x = x_ref[0]
    r = x.shape[0]
    nin = S5_T * LANE
    half = S5_GPT * B_STATE
    ye = _dot(x, win_ref[0])
    s_re = ye[:, nin:nin + half]
    s_im = ye[:, nin + half:nin + 2 * half]
    rows = lax.broadcasted_iota(I32, (r, half), 0)
    for k in range(n_steps):
        d = 1 << k
        if d >= r:
            break
        a_re = ap_ref[0, 2 * k:2 * k + 1, :]
        a_im = ap_ref[0, 2 * k + 1:2 * k + 2, :]
        keep = rows >= d
        p_re = jnp.where(keep, pltpu.roll(s_re, d, 0), 0.0)
        p_im = jnp.where(keep, pltpu.roll(s_im, d, 0), 0.0)
        s_re, s_im = s_re + a_re * p_re - a_im * p_im, s_im + a_re * p_im + a_im * p_re
    keep = rows >= 1
    sp_re = jnp.where(keep, pltpu.roll(s_re, 1, 0), 0.0)
    sp_im = jnp.where(keep, pltpu.roll(s_im, 1, 0), 0.0)
    y = (ye[:, 0:nin] + _dot(sp_re.astype(BF16), wout_ref[0, 0:half, :])
         + _dot(sp_im.astype(BF16), wout_ref[0, half:2 * half, :]))
    o_ref[0] = (y + dsk_ref[0] * x.astype(F32)).astype(o_ref.dtype)


def _s5(u_lt, w_in, w_out, lam_log, d_skip, nb, seq):
    nt, t, _ = u_lt.shape
    r = seq // S5_T
    n_steps = max(1, (r - 1).bit_length())
    apow = _s5_apow(lam_log, n_steps)
    x = u_lt.reshape(nt, t // S5_T, S5_T * LANE)
    dsk = jnp.tile(d_skip.astype(F32).reshape(nt, 1, LANE), (1, 1, S5_T))
    nin = S5_T * LANE
    half = S5_GPT * B_STATE
    y = pl.pallas_call(
        functools.partial(_s5_kernel, n_steps=n_steps),
        out_shape=jax.ShapeDtypeStruct(x.shape, BF16),
        grid=(nt, nb),
        in_specs=[
            pl.BlockSpec((1, r, nin), lambda j, b: (j, b, 0)),
            pl.BlockSpec((1, nin, nin + 2 * half), lambda j, b: (j, 0, 0)),
            pl.BlockSpec((1, 2 * half, nin), lambda j, b: (j, 0, 0)),
            pl.BlockSpec((1, 2 * n_steps, half), lambda j, b: (j, 0, 0)),
            pl.BlockSpec((1, 1, nin), lambda j, b: (j, 0, 0)),
        ],
        out_specs=pl.BlockSpec((1, r, nin), lambda j, b: (j, b, 0)),
        compiler_params=_cp(("parallel", "parallel")),
        name="s5",
    )(x, w_in, w_out, apow, dsk)
    return y.reshape(nt, t, LANE)


def _hgrn_levels():
    nsub = C_CHUNK // C_SUB
    levels = []
    bs = 2
    while bs <= nsub:
        levels.append((bs, [(i // bs) * bs + bs // 2 - 1 for i in range(nsub)]))
        bs *= 2
    return levels


def _hgrn_kernel(q_ref, f_ref, i_ref, og_ref, lb_ref, ng_ref, o_ref, st_ref):
    cpb = q_ref.shape[1] // C_CHUNK
    nsub = C_CHUNK // C_SUB
    dh = C_HEAD_DIM

    @pl.when(pl.program_id(2) == 0)
    def _():
        st_ref[...] = jnp.zeros(st_ref.shape, F32)

    ng = ng_ref[...]
    row = lax.broadcasted_iota(I32, (C_CHUNK, dh), 0)
    rsub = row % C_SUB
    sub_of_row = row // C_SUB
    r2 = lax.broadcasted_iota(I32, (C_CHUNK, C_CHUNK), 0)
    c2 = lax.broadcasted_iota(I32, (C_CHUNK, C_CHUNK), 1)
    mask0 = (r2 // C_SUB == c2 // C_SUB) & (r2 >= c2)
    levels = _hgrn_levels()

    def chunk(ci, carry):
        for hh in range(q_ref.shape[0]):
            head_chunk(ci, hh)
        return carry

    def head_chunk(ci, hh):
        sl = pl.ds(pl.multiple_of(ci * C_CHUNK, C_CHUNK), C_CHUNK)
        lb = lb_ref[hh]
        q = _silu(q_ref[hh, sl, :].astype(F32))
        forget = lb + (1.0 - lb) * _sigmoid(f_ref[hh, sl, :].astype(F32))
        k = 1.0 - forget
        v = i_ref[hh, sl, :].astype(F32)
        g = jnp.log(forget)
        loc = g
        d = 1
        while d < C_SUB:
            loc = loc + jnp.where(rsub >= d, pltpu.roll(loc, d, 0), 0.0)
            d *= 2
        loc3 = loc.reshape(nsub, C_SUB, dh)
        last = [loc3[s, C_SUB - 1:C_SUB, :] for s in range(nsub)]
        pre = [jnp.zeros((1, dh), F32)]
        for s in range(nsub - 1):
            pre.append(pre[s] + last[s])
        ends = [pre[s] + last[s] for s in range(nsub)]
        bc = (loc3 + jnp.stack(pre, axis=0)).reshape(C_CHUNK, dh)
        total = ends[nsub - 1]

        def bcast_rows(vals):
            return jnp.broadcast_to(jnp.stack(vals, axis=0), (nsub, C_SUB, dh)).reshape(C_CHUNK, dh)

        ref0 = bcast_rows(pre)
        qd = (q * jnp.exp(bc - ref0)).astype(BF16)
        kd = (k * jnp.exp(ref0 - bc)).astype(BF16)
        scores = jnp.where(mask0, _dot_nt(qd, kd), 0.0)
        for bs, ref_sub in levels:
            ref = bcast_rows([ends[s] for s in ref_sub])
            upper = (sub_of_row % bs) >= (bs // 2)
            qd = (q * jnp.exp(jnp.where(upper, bc - ref, NEG))).astype(BF16)
            kd = (k * jnp.exp(jnp.where(upper, NEG, ref - bc))).astype(BF16)
            s_l = _dot_nt(qd, kd)
            if bs < nsub:
                s_l = jnp.where(r2 // (bs * C_SUB) == c2 // (bs * C_SUB), s_l, 0.0)
            scores = scores + s_l
        vb = v.astype(BF16)
        st_t = st_ref[hh]
        q_in = (q * jnp.exp(bc)).astype(BF16)
        o = _dot(scores.astype(BF16), vb) + _dot_nt(q_in, st_t.astype(BF16))
        k_end = (k * jnp.exp(total - bc)).astype(BF16)
        st_ref[hh] = st_t * jnp.exp(total) + _dot(v.T.astype(BF16), k_end)
        ms = jnp.mean(o * o, axis=-1, keepdims=True)
        out = o * lax.rsqrt(ms + RMS_EPS) * ng * _silu(og_ref[hh, sl, :].astype(F32))
        o_ref[hh, sl, :] = out.astype(o_ref.dtype)

    lax.fori_loop(0, cpb, chunk, 0)


def _hgrn(p_lt, lower_bound, norm_g, nb, seq, *, tl=1024, hps=4):
    nh = C_HEADS
    t = nb * seq
    tl = min(tl, seq)
    spb = seq // tl
    blk = lambda off: pl.BlockSpec((hps, tl, LANE), lambda b, h, s: (off // hps + h, b * spb + s, 0))
    return pl.pallas_call(
        _hgrn_kernel,
        out_shape=jax.ShapeDtypeStruct((nh, t, LANE), BF16),
        grid=(nb, nh // hps, spb),
        in_specs=[blk(0), blk(nh), blk(2 * nh), blk(3 * nh),
                  pl.BlockSpec((hps, 1, LANE), lambda b, h, s: (h, 0, 0)),
                  pl.BlockSpec((1, LANE), lambda b, h, s: (0, 0))],
        out_specs=pl.BlockSpec((hps, tl, LANE), lambda b, h, s: (h, b * spb + s, 0)),
        scratch_shapes=[pltpu.VMEM((hps, C_HEAD_DIM, C_HEAD_DIM), F32)],
        compiler_params=_cp(("parallel", "parallel", "arbitrary")),
        name="hgrn2",
    )(p_lt, p_lt, p_lt, p_lt, lower_bound.astype(F32).reshape(nh, 1, LANE), norm_g.astype(F32)[None, :])


def _pack_bf16_pair(lo, hi):
    lo_b = pltpu.bitcast(lo.astype(BF16).astype(F32), U32)
    hi_b = pltpu.bitcast(hi.astype(BF16).astype(F32), U32)
    return (hi_b & jnp.uint32(0xFFFF0000)) | (lo_b >> 16)


def _unpack_bf16_pair(u):
    lo = pltpu.bitcast(u << 16, F32)
    hi = pltpu.bitcast(u & jnp.uint32(0xFFFF0000), F32)
    return lo, hi


ROW_SUB = 8


def _store_rows_tiled(ref, v):
    n = v.shape[0]
    for s in range(ROW_SUB):
        ref[pl.ds(s, n, stride=ROW_SUB), :] = v[:, s * LANE:(s + 1) * LANE]


def _load_rows_tiled(ref, n):
    return [ref[pl.ds(s, n, stride=ROW_SUB), :] for s in range(ROW_SUB)]


def _router_kernel(h_ref, g_ref, sh_ref, sc_ref, rw_ref, rb_ref, xq_ref, ti_ref, tw_ref, rk_ref, cnt_ref, run_ref):
    @pl.when((pl.program_id(0) == 0) & (pl.program_id(1) == 0))
    def _():
        run_ref[...] = jnp.zeros(run_ref.shape, F32)

    hn = _norm_mod(h_ref[0], g_ref[...], sh_ref[0], sc_ref[0])
    half = hn.shape[1] // 2
    _store_rows_tiled(xq_ref, _pack_bf16_pair(hn[:, :half], hn[:, half:]))
    logits = _dot_hi(hn, rw_ref[...]) + rb_ref[...]
    lane = lax.broadcasted_iota(I32, logits.shape, 1)
    lane_f = lane.astype(F32)
    vals, idxs = [], []
    for _ in range(TOP_K):
        m = jnp.max(logits, axis=-1, keepdims=True)
        idx = jnp.min(jnp.where(logits == m, lane_f, float(LANE)), axis=-1, keepdims=True)
        vals.append(m)
        idxs.append(idx)
        logits = jnp.where(lane_f == idx, NEG, logits)
    exps = [jnp.exp(v - vals[0]) for v in vals]
    den = exps[0]
    for e in exps[1:]:
        den = den + e
    tm = logits.shape[0]
    hot = [lane_f == idxs[k] for k in range(TOP_K)]
    oh = jnp.where(hot[0] | hot[1] | hot[2] | hot[3], 1.0, 0.0)
    r2 = lax.broadcasted_iota(I32, (tm, tm), 0)
    c2 = lax.broadcasted_iota(I32, (tm, tm), 1)
    before = jnp.where(r2 > c2, 1.0, 0.0).astype(BF16)
    pref = _dot(before, oh.astype(BF16)) + run_ref[...]
    ti = jnp.zeros(lane.shape, F32)
    tw = jnp.zeros(lane.shape, F32)
    rk = jnp.zeros(lane.shape, F32)
    for k in range(TOP_K):
        ti = jnp.where(lane == k, idxs[k], ti)
        tw = jnp.where(lane == k, exps[k] / den, tw)
        rk = jnp.where(lane == k, jnp.sum(jnp.where(hot[k], pref, 0.0), axis=-1, keepdims=True), rk)
    ti_ref[...] = ti.astype(I32)
    tw_ref[...] = tw
    rk_ref[...] = rk.astype(I32)
    run_ref[...] = run_ref[...] + jnp.sum(oh, axis=0, keepdims=True)
    cnt_ref[...] = run_ref[...]


def _router(h, g, sh, sc, rw, rb, *, tm=512):
    nb, seq, d = h.shape
    tm = min(tm, seq)
    rpb = seq // tm
    t = nb * seq
    rwp = jnp.zeros((d, LANE), F32).at[:, :N_EXPERTS].set(rw.astype(F32))
    rbp = jnp.full((1, LANE), NEG, F32).at[0, :N_EXPERTS].set(rb.astype(F32))
    row = lambda b, i: (b * rpb + i, 0)
    return pl.pallas_call(
        _router_kernel,
        out_shape=[jax.ShapeDtypeStruct((t * ROW_SUB, LANE), U32),
                   jax.ShapeDtypeStruct((t, LANE), I32),
                   jax.ShapeDtypeStruct((t, LANE), F32),
                   jax.ShapeDtypeStruct((t, LANE), I32),
                   jax.ShapeDtypeStruct((1, LANE), F32)],
        grid=(nb, rpb),
        in_specs=[
            pl.BlockSpec((1, tm, d), lambda b, i: (b, i, 0)),
            pl.BlockSpec((1, d), lambda b, i: (0, 0)),
            pl.BlockSpec((1, 1, d), lambda b, i: (b, 0, 0)),
            pl.BlockSpec((1, 1, d), lambda b, i: (b, 0, 0)),
            pl.BlockSpec((d, LANE), lambda b, i: (0, 0)),
            pl.BlockSpec((1, LANE), lambda b, i: (0, 0)),
        ],
        out_specs=[pl.BlockSpec((tm * ROW_SUB, LANE), row), pl.BlockSpec((tm, LANE), row), pl.BlockSpec((tm, LANE), row),
                   pl.BlockSpec((tm, LANE), row), pl.BlockSpec((1, LANE), lambda b, i: (0, 0))],
        scratch_shapes=[pltpu.VMEM((1, LANE), F32)],
        compiler_params=_cp(("arbitrary", "arbitrary")),
        name="moe_router",
    )(h, g[None, :], sh[:, None, :], sc[:, None, :], rwp, rbp)


def _route_plan(top_i, rank, counts, tm):
    t = top_i.shape[0]
    na = t * TOP_K
    nt = na // tm + N_EXPERTS + 1
    ptiles = (counts + tm - 1) // tm
    tile_end = jnp.cumsum(ptiles)
    tile_start = tile_end - ptiles
    onehot = top_i[:, :, None] == jnp.arange(N_EXPERTS, dtype=I32)[None, None, :]
    start = jnp.sum(jnp.where(onehot, tile_start[None, None, :], 0), axis=-1)
    pos = (start * tm + rank).reshape(na)
    asg = jnp.full((nt * tm,), -1, I32).at[pos].set(jnp.arange(na, dtype=I32), unique_indices=True)
    tiles_done = jnp.sum((tile_end[None, :] <= jnp.arange(nt, dtype=I32)[:, None]).astype(I32), axis=1)
    tile_expert = jnp.minimum(tiles_done, N_EXPERTS - 1).astype(I32)
    n_valid = tile_end[-1:].astype(I32)
    asg = asg.reshape(nt, tm)
    src = jnp.maximum(asg, 0) // TOP_K
    dst = jnp.where(asg >= 0, (asg % TOP_K) * t + asg // TOP_K, na + jnp.arange(tm, dtype=I32)[None, :])
    return jnp.concatenate([src, dst], axis=1) * ROW_SUB, tile_expert, n_valid


def _w1_prep_kernel(w_ref, p_ref, o_ref):
    nblk = o_ref.shape[1]
    wblk = o_ref.shape[3]
    for b in range(nblk):
        o_ref[0, b] = _dot(w_ref[0, 0, :, b * wblk:(b + 1) * wblk].astype(BF16), p_ref[...]).astype(BF16)


def _w1_prep(w1_all, layer, *, tr=512):
    _, e, d, n2 = w1_all.shape
    wblk = 2 * LANE
    nblk = n2 // wblk
    r = jnp.arange(wblk)
    perm = (jnp.where(r % 2 == 0, r // 2, LANE + r // 2)[:, None] == jnp.arange(wblk)[None, :]).astype(BF16)
    return pl.pallas_call(
        _w1_prep_kernel,
        out_shape=jax.ShapeDtypeStruct((e, nblk, d, wblk), BF16),
        grid=(e, d // tr),
        in_specs=[pl.BlockSpec((1, 1, tr, n2), lambda x, i: (layer, x, i, 0)),
                  pl.BlockSpec((wblk, wblk), lambda x, i: (0, 0))],
        out_specs=pl.BlockSpec((1, nblk, tr, wblk), lambda x, i: (x, 0, i, 0)),
        compiler_params=_cp(("parallel", "parallel")),
        name="moe_w1_prep",
    )(w1_all, perm)


def _experts_kernel(te_ref, nv_ref, asg_hbm, x_hbm, w1_ref, b1_ref, w2_ref, b2_ref, out_hbm,
                    asg_sm, xbuf, ybuf, sem_idx, sem_g, sem_s, *, n_tok):
    del te_ref
    i = pl.program_id(0)
    nt = pl.num_programs(0)
    nv = nv_ref[0]
    rs = ROW_SUB
    tm = xbuf.shape[1] // rs
    half = rs * LANE
    nblk = w1_ref.shape[1]
    na = TOP_K * n_tok


    def asg_copy(tile, slot):
        return pltpu.make_async_copy(asg_hbm.at[tile], asg_sm.at[slot], sem_idx.at[slot])

    def buf_rows(r):
        return pl.ds(r * rs, rs) if isinstance(r, int) else pl.ds(pl.multiple_of(r * rs, rs), rs)

    def gather_start(islot, xslot, r):
        src = pl.multiple_of(asg_sm[islot, r], rs)
        pltpu.make_async_copy(x_hbm.at[pl.ds(src, rs)], xbuf.at[xslot, buf_rows(r)], sem_g.at[xslot]).start()

    def scatter_start(islot, yslot, r):
        dst = pl.multiple_of(asg_sm[islot, tm + r], rs)
        pltpu.make_async_copy(ybuf.at[yslot, buf_rows(r)], out_hbm.at[pl.ds(dst, rs)], sem_s.at[yslot]).start()

    def wait_gathers(xslot):
        pltpu.make_async_copy(x_hbm.at[pl.ds(0, tm * rs)], xbuf.at[xslot], sem_g.at[xslot]).wait()

    def wait_scatters(yslot):
        pltpu.make_async_copy(ybuf.at[yslot], out_hbm.at[pl.ds(0, tm * rs)], sem_s.at[yslot]).wait()

    @pl.when(i == 0)
    def _():
        ybuf[...] = jnp.zeros(ybuf.shape, U32)
        init = pltpu.make_async_copy(ybuf.at[0], out_hbm.at[pl.ds(na * rs, tm * rs)], sem_s.at[0])
        init.start()
        init.wait()

        def fill(r, c):
            asg_sm[3, tm + r] = (na + r) * rs
            return c
        lax.fori_loop(0, tm, fill, 0)
        first = asg_copy(0, 0)
        first.start()
        first.wait()

        def body(r, c):
            gather_start(0, 0, r)
            return c
        lax.fori_loop(0, tm, body, 0)
        asg_copy(1, 1).start()

    @pl.when(i < nv)
    def _():
        slot = i % 2
        asg_copy(i + 1, (i + 1) % 4).wait()

        @pl.when(i + 2 <= nv)
        def _():
            asg_copy(i + 2, (i + 2) % 4).start()

        wait_gathers(slot)
        pieces = [_unpack_bf16_pair(p) for p in _load_rows_tiled(xbuf.at[slot], tm)]
        xb = jnp.concatenate([p[0].astype(BF16) for p in pieces] + [p[1].astype(BF16) for p in pieces], axis=1)
        g_slot, s_slot = (i + 1) % 4, (i + 3) % 4
        rows_per_blk = -(-tm // nblk)
        acts = []
        zmask = (nv >> 30).astype(U32)
        for b in range(nblk):
            for r in range(b * rows_per_blk, min((b + 1) * rows_per_blk, tm)):
                gather_start(g_slot, 1 - slot, r)
                scatter_start(s_slot, 1 - slot, r)
            probe = pltpu.bitcast(xbuf[slot, 0:1, :] & zmask, F32)
            gu = _dot(xb, w1_ref[0, b]) + b1_ref[0, b]
            gate = jnp.minimum(gu[:, 0:LANE] + probe, SWIGLU_LIMIT)
            up = jnp.clip(gu[:, LANE:2 * LANE], -SWIGLU_LIMIT, SWIGLU_LIMIT)
            acts.append(((up + 1.0) * (gate * _sigmoid(SWIGLU_ALPHA * gate))).astype(BF16))
        y = _dot(jnp.concatenate(acts, axis=1), w2_ref[0]) + b2_ref[0]

        @pl.when(i >= 1)
        def _():
            wait_scatters(slot)

        _store_rows_tiled(ybuf.at[slot], _pack_bf16_pair(y[:, :half], y[:, half:]))

    @pl.when(i == nv)
    def _():
        wait_gathers(nv % 2)
        yslot = (nv + 1) % 2
        islot = (nv + 3) % 4

        def body(r, c):
            scatter_start(islot, yslot, r)
            return c
        lax.fori_loop(0, tm, body, 0)
        wait_scatters(0)
        wait_scatters(1)


def _experts(xq, asg, tile_expert, n_valid, w1p, b1p, w2, b2, tm):
    t = xq.shape[0] // ROW_SUB
    nt = asg.shape[0]
    d = 2 * ROW_SUB * LANE
    nblk, wblk = w1p.shape[1], w1p.shape[3]
    e_map4 = lambda i, te, nv: (te[i], 0, 0, 0)
    grid_spec = pltpu.PrefetchScalarGridSpec(
        num_scalar_prefetch=2,
        grid=(nt,),
        in_specs=[
            pl.BlockSpec(memory_space=pl.ANY),
            pl.BlockSpec(memory_space=pl.ANY),
            pl.BlockSpec((1, nblk, d, wblk), e_map4),
            pl.BlockSpec((1, nblk, 1, wblk), e_map4),
            pl.BlockSpec((1, nblk * LANE, d), lambda i, te, nv: (te[i], 0, 0)),
            pl.BlockSpec((1, 1, d), lambda i, te, nv: (te[i], 0, 0)),
        ],
        out_specs=pl.BlockSpec(memory_space=pl.ANY),
        scratch_shapes=[
            pltpu.SMEM((4, 2 * tm), I32),
            pltpu.VMEM((2, tm * ROW_SUB, LANE), U32),
            pltpu.VMEM((2, tm * ROW_SUB, LANE), U32),
            pltpu.SemaphoreType.DMA((4,)),
            pltpu.SemaphoreType.DMA((2,)),
            pltpu.SemaphoreType.DMA((2,)),
        ],
    )
    return pl.pallas_call(
        functools.partial(_experts_kernel, n_tok=t),
        out_shape=jax.ShapeDtypeStruct(((TOP_K * t + tm) * ROW_SUB, LANE), U32),
        grid_spec=grid_spec,
        compiler_params=_cp(("arbitrary",)),
        name="moe_experts",
    )(tile_expert, n_valid, asg, xq, w1p, b1p, w2, b2)


def _combine_kernel(y0_ref, y1_ref, y2_ref, y3_ref, tw_ref, h_ref, gate_ref, o_ref):
    tm = tw_ref.shape[0]
    half = ROW_SUB * LANE
    ws = [tw_ref[:, k:k + 1] for k in range(TOP_K)]
    for s in range(ROW_SUB):
        acc_lo = acc_hi = None
        for k, y_ref in enumerate((y0_ref, y1_ref, y2_ref, y3_ref)):
            lo, hi = _unpack_bf16_pair(y_ref[pl.ds(s, tm, stride=ROW_SUB), :])
            acc_lo = lo * ws[k] if acc_lo is None else acc_lo + lo * ws[k]
            acc_hi = hi * ws[k] if acc_hi is None else acc_hi + hi * ws[k]
        c0, c1 = s * LANE, half + s * LANE
        o_ref[0, :, c0:c0 + LANE] = h_ref[0, :, c0:c0 + LANE] + gate_ref[0, :, c0:c0 + LANE] * acc_lo
        o_ref[0, :, c1:c1 + LANE] = h_ref[0, :, c1:c1 + LANE] + gate_ref[0, :, c1:c1 + LANE] * acc_hi


def _combine(y4, tw, h, gate, *, tm=512):
    nb, seq, d = h.shape
    tm = min(tm, seq)
    rpb = seq // tm
    t = nb * seq
    nblk = t // tm
    yspec = lambda k: pl.BlockSpec((tm * ROW_SUB, LANE), lambda b, i: (k * nblk + b * rpb + i, 0))
    return pl.pallas_call(
        _combine_kernel,
        out_shape=jax.ShapeDtypeStruct((nb, seq, d), F32),
        grid=(nb, rpb),
        in_specs=[yspec(0), yspec(1), yspec(2), yspec(3),
                  pl.BlockSpec((tm, LANE), lambda b, i: (b * rpb + i, 0)),
                  pl.BlockSpec((1, tm, d), lambda b, i: (b, i, 0)),
                  pl.BlockSpec((1, 1, d), lambda b, i: (b, 0, 0))],
        out_specs=pl.BlockSpec((1, tm, d), lambda b, i: (b, i, 0)),
        compiler_params=_cp(("parallel", "parallel")),
        name="moe_combine",
    )(y4, y4, y4, y4, tw, h, gate[:, None, :])


def _moe_layer(h, g, sh, sc, gate, rw, rb, w1_all, layer, b1, w2, b2):
    xq, ti, tw, rk, cnt = _router(h, g, sh, sc, rw, rb)
    counts = cnt[0, :N_EXPERTS].astype(I32)
    asg, tile_expert, n_valid = _route_plan(ti[:, :TOP_K], rk[:, :TOP_K], counts, MOE_TM)
    ne, ff, d = w2.shape
    nblk = ff // LANE
    w1p = _w1_prep(w1_all, layer)
    b1p = b1.astype(F32).reshape(ne, nblk, LANE, 2).transpose(0, 1, 3, 2).reshape(ne, nblk, 1, 2 * LANE)
    w2p = w2.astype(BF16)
    y4 = _experts(xq, asg, tile_expert, n_valid, w1p, b1p, w2p, b2[:, None, :].astype(F32), MOE_TM)
    return _combine(y4, tw, h, gate)


def _final_kernel(h_ref, g_ref, sh_ref, sc_ref, o_ref):
    o_ref[0] = _norm_mod(h_ref[0], g_ref[...], sh_ref[0], sc_ref[0])


def _final(h, g, sh, sc, *, tm=512):
    nb, seq, d = h.shape
    tm = min(tm, seq)
    return pl.pallas_call(
        _final_kernel,
        out_shape=jax.ShapeDtypeStruct((nb, seq, d), F32),
        grid=(nb, seq // tm),
        in_specs=[pl.BlockSpec((1, tm, d), lambda b, i: (b, i, 0)),
                  pl.BlockSpec((1, d), lambda b, i: (0, 0)),
                  pl.BlockSpec((1, 1, d), lambda b, i: (b, 0, 0)),
                  pl.BlockSpec((1, 1, d), lambda b, i: (b, 0, 0))],
        out_specs=pl.BlockSpec((1, tm, d), lambda b, i: (b, i, 0)),
        compiler_params=_cp(("parallel", "parallel")),
        name="final_norm",
    )(h, g[None, :], sh[:, None, :], sc[:, None, :])


def _ab_mixer(h, g, sh, sc, gate, w_in, conv_w, conv_b, a_log, dt_bias, ssd_d, ssd_norm_g,
              lam_re, lam_im, log_step, b_re, b_im, c_re, c_im, s5_d, glu_w, glu_b, w_out):
    nb, seq, d = h.shape
    n_zxbc = A_INNER + A_INNER + 2 * A_GROUPS * A_STATE
    w_zxbc = w_in[:, :n_zxbc].astype(BF16)
    tail_t = lax.optimization_barrier(w_in[:, n_zxbc:].astype(BF16).T)
    w_dt = jnp.pad(tail_t[:A_HEADS], ((0, LANE - A_HEADS), (0, 0))).T
    w_u = lax.optimization_barrier(tail_t[A_HEADS:]).T
    zxbc, dt_raw = _proj(h, g, sh, sc, w_zxbc, out_lt=False, w_extra=w_dt)
    u_lt = _proj(h, g, sh, sc, w_u, out_lt=True)
    y_a = _ssd(zxbc, dt_raw, conv_w, conv_b, a_log, dt_bias, ssd_d, ssd_norm_g, nb, seq)
    s5_in, s5_out, lam_log = _s5_prep(lam_re, lam_im, log_step, b_re, b_im, c_re, c_im)
    y_b = _s5(u_lt, s5_in, s5_out, lam_log, s5_d, nb, seq)
    y_b = _glu(y_b, glu_w.astype(BF16), glu_b.astype(F32))
    return _outproj([y_a, y_b], [False, False], w_out.astype(BF16), h, gate)


def _hgrn_mixer(h, g, sh, sc, gate, w_in, lower_bound, norm_g, w_out):
    nb, seq, _ = h.shape
    p_lt = _proj(h, g, sh, sc, w_in.astype(BF16), out_lt=True)
    o_lt = _hgrn(p_lt, lower_bound, norm_g, nb, seq)
    return _outproj([o_lt], [True], w_out.astype(BF16), h, gate)


def kernel(x, c, ada_w, ada_b, norm1_g, norm2_g, ab_w_in, ab_conv_w, ab_conv_b, ssd_a_log, ssd_dt_bias, ssd_d, ssd_norm_g, s5_lam_re, s5_lam_im, s5_log_step, s5_b_re, s5_b_im, s5_c_re, s5_c_im, s5_d, s5_glu_w, s5_glu_b, ab_w_out, hg_w_in, hg_lower_bounds, hg_norm_g, hg_w_out, moe_router_w, moe_router_b, moe_w1, moe_b1, moe_w2, moe_b2, final_ada_w, final_ada_b, final_norm_g):
    depth = ada_w.shape[0]
    d = x.shape[-1]
    mods = _ada(c, ada_w, ada_b)
    fmod = _ada(c, final_ada_w[None], final_ada_b[None])[0]
    lb_soft = jax.nn.softmax(hg_lower_bounds.astype(F32), axis=0)
    lbs = jnp.cumsum(lb_soft, axis=0) - lb_soft[0]
    h = x
    for l in range(depth):
        sh1, sc1, g1, sh2, sc2, g2 = [mods[l][:, k * d:(k + 1) * d] for k in range(6)]
        i = l // 2
        if l % 2 == 0:
            h = _ab_mixer(h, norm1_g[l], sh1, sc1, g1, ab_w_in[i], ab_conv_w[i], ab_conv_b[i], ssd_a_log[i],
                          ssd_dt_bias[i], ssd_d[i], ssd_norm_g[i], s5_lam_re[i], s5_lam_im[i], s5_log_step[i],
                          s5_b_re[i], s5_b_im[i], s5_c_re[i], s5_c_im[i], s5_d[i], s5_glu_w[i], s5_glu_b[i],
                          ab_w_out[i])
        else:
            h = _hgrn_mixer(h, norm1_g[l], sh1, sc1, g1, hg_w_in[i], lbs[l], hg_norm_g[i], hg_w_out[i])
        h = _moe_layer(h, norm2_g[l], sh2, sc2, g2, moe_router_w[l], moe_router_b[l], moe_w1, l, moe_b1[l],
                       moe_w2[l], moe_b2[l])
    return _final(h, final_norm_g, fmod[:, :d], fmod[:, d:])
```
